```python
import numpy as np
import jax
import jax.numpy as jnp
from jax import lax

D_MODEL = 1024
BATCH = 4
SEQ = 8192
DEPTH = 4

CHUNK = 64
QBLK = 128
N_GROUPS = 4
GROUP_WIDTH = D_MODEL // N_GROUPS
HEAD_DIM = 64
N_HEADS = GROUP_WIDTH // HEAD_DIM
MIX_WIDTH = N_GROUPS * GROUP_WIDTH
SHORT_CONV = 4
FFN_CONV = 3
D_FF = ((8 * D_MODEL // 3 + 127) // 128) * 128
ROPE_BASE = 10000.0
RET_DECAY_EXP = 5.0
EPS = 1e-6
NEG_BIG = -1e30
PROJ_SIZES = (GROUP_WIDTH,) * 4 + (N_HEADS, N_HEADS) + (GROUP_WIDTH,) * 4 + (GROUP_WIDTH,) * 4 + (N_HEADS,) + (GROUP_WIDTH,) * 4
PROJ_WIDTH = sum(PROJ_SIZES)
F32 = jnp.float32

kernel_name = 'hybrid_parallel_group_streaming_encoder'


def _rmsnorm(x, g):
    xf = x.astype(F32)
    y = xf * lax.rsqrt(jnp.mean(xf * xf, axis=-1, keepdims=True) + EPS)
    return (y * g.astype(F32)).astype(x.dtype)


def _head_rms(t, g):
    t = t.astype(F32)
    return t * lax.rsqrt(jnp.mean(t * t, axis=-1, keepdims=True) + EPS) * g.astype(F32)


def _head_ln(t, g):
    t = t.astype(F32)
    tc = t - jnp.mean(t, axis=-1, keepdims=True)
    return tc * lax.rsqrt(jnp.mean(tc * tc, axis=-1, keepdims=True) + EPS) * g.astype(F32)


def _l2norm(t):
    return t * lax.rsqrt(jnp.sum(t * t, axis=-1, keepdims=True) + EPS)


def _masked_exp(mask, d):
    return jnp.where(mask, jnp.exp(jnp.where(mask, d, 0.0)), 0.0)


def _heads(t):
    b, s, _ = t.shape
    return t.reshape(b, s, -1, HEAD_DIM).transpose(0, 2, 1, 3)


def _merge(t):
    b, h, s, d = t.shape
    return t.transpose(0, 2, 1, 3).reshape(b, s, h * d)


def _chunks(t):
    return t.reshape(t.shape[:2] + (t.shape[2] // CHUNK, CHUNK) + t.shape[3:])


def _unchunk(t):
    return t.reshape(t.shape[:2] + (t.shape[2] * t.shape[3],) + t.shape[4:])


def _causal_dwconv(x, w):
    width, ch = w.shape
    return lax.conv_general_dilated(x, w.astype(x.dtype).reshape(width, 1, ch), window_strides=(1,), padding=((width - 1, 0),), dimension_numbers=('NWC', 'WIO', 'NWC'), feature_group_count=ch)


def _rope(t, cos, sin):
    t1, t2 = jnp.split(t, 2, axis=-1)
    return jnp.concatenate([t1 * cos - t2 * sin, t1 * sin + t2 * cos], axis=-1)


def _scan_states(decay, upd):
    def step(state, inp):
        a, u = inp
        return a * state + u, state
    _, prev = lax.scan(step, jnp.zeros_like(upd[:, :, 0]), (jnp.moveaxis(decay, 2, 0), jnp.moveaxis(upd, 2, 0)))
    return jnp.moveaxis(prev, 0, 2)


def _masks():
    ones = jnp.ones((CHUNK, CHUNK), dtype=bool)
    return jnp.tril(ones), jnp.tril(ones, -1)


def _gated_deltanet(q, k, v, gate, b, a, conv_w, a_log, dt_bias, onorm):
    tril, strict = _masks()
    qkv = jax.nn.silu(_causal_dwconv(jnp.concatenate([q, k, v], axis=-1), conv_w).astype(F32))
    q, k, v = (_heads(t) for t in jnp.split(qkv, 3, axis=-1))
    q = _l2norm(q) * HEAD_DIM ** -0.5
    k = _l2norm(k)
    beta = _chunks(jax.nn.sigmoid(b.astype(F32)).transpose(0, 2, 1))[..., None]
    log_alpha = -jnp.exp(a_log.astype(F32)) * jax.nn.softplus(a.astype(F32) + dt_bias.astype(F32))
    G = jnp.cumsum(_chunks(log_alpha.transpose(0, 2, 1)), axis=-1)
    qc, kc, vc = _chunks(q), _chunks(k), _chunks(v)
    gam = _masked_exp(tril, G[..., :, None] - G[..., None, :])
    kb = kc * beta
    a_mat = jnp.where(strict, jnp.einsum('bhntd,bhnsd->bhnts', kb, kc) * gam, 0.0)
    rhs = jnp.concatenate([vc * beta, kb * jnp.exp(G)[..., None]], axis=-1)
    sol = lax.linalg.triangular_solve(a_mat + jnp.eye(CHUNK, dtype=F32), rhs, left_side=True, lower=True, unit_diagonal=True)
    u, w = jnp.split(sol, 2, axis=-1)
    qk = jnp.einsum('bhntd,bhnsd->bhnts', qc, kc) * gam
    qg = qc * jnp.exp(G)[..., None]
    kd = kc * jnp.exp(G[..., -1:] - G)[..., None]
    gend = jnp.exp(G[..., -1])[..., None, None]

    def step(state, inp):
        u_i, w_i, qk_i, qg_i, kd_i, ge_i = inp
        v_new = u_i - jnp.einsum('bhtk,bhkv->bhtv', w_i, state)
        o_i = jnp.einsum('bhtk,bhkv->bhtv', qg_i, state) + jnp.einsum('bhts,bhsv->bhtv', qk_i, v_new)
        state = ge_i * state + jnp.einsum('bhsk,bhsv->bhkv', kd_i, v_new)
        return state, o_i

    xs = tuple(jnp.moveaxis(t, 2, 0) for t in (u, w, qk, qg, kd, gend))
    state0 = jnp.zeros(qc.shape[:2] + (HEAD_DIM, HEAD_DIM), F32)
    _, o = lax.scan(step, state0, xs)
    o = _unchunk(jnp.moveaxis(o, 0, 2))
    return _merge(_head_rms(o, onorm)) * jax.nn.silu(gate.astype(F32))


def _retention(q, k, v, gate, onorm, cos, sin):
    tril, _ = _masks()
    q = _rope(_heads(q).astype(F32), cos, sin)
    k = _rope(_heads(k).astype(F32), cos, sin) * HEAD_DIM ** -0.5
    v = _heads(v).astype(F32)
    lgh = jnp.log1p(-jnp.exp2(-RET_DECAY_EXP - jnp.arange(N_HEADS, dtype=F32)))
    n = jnp.arange(CHUNK, dtype=F32)
    dmat = _masked_exp(tril[None], (n[:, None] - n[None, :])[None] * lgh[:, None, None])
    w_end = jnp.exp((CHUNK - 1 - n)[None, :] * lgh[:, None])[None, :, None, :, None]
    w_start = jnp.exp((n + 1)[None, :] * lgh[:, None])[None, :, None, :, None]
    qc, kc, vc = _chunks(q), _chunks(k), _chunks(v)
    scores = jnp.einsum('bhntd,bhnsd->bhnts', qc, kc) * dmat[None, :, None]
    inner = jnp.einsum('bhnts,bhnsd->bhntd', scores, vc)
    upd = jnp.einsum('bhnsk,bhnsv->bhnkv', kc * w_end, vc)
    decay = jnp.broadcast_to(jnp.exp(CHUNK * lgh)[None, :, None, None, None], (1, N_HEADS, qc.shape[2], 1, 1))
    prev = _scan_states(decay, upd)
    cross = jnp.einsum('bhntk,bhnkv->bhntv', qc * w_start, prev)
    o = _unchunk(inner + cross)
    return _merge(_head_ln(o, onorm)) * jax.nn.silu(gate.astype(F32))


def _forgetting_attention(q, k, v, gate, f, qnorm, knorm, fbias):
    q = _head_rms(_heads(q), qnorm) * HEAD_DIM ** -0.5
    k = _head_rms(_heads(k), knorm)
    v = _heads(v).astype(F32)
    logf = jax.nn.log_sigmoid(f.astype(F32) + fbias.astype(F32)).transpose(0, 2, 1)
    c = jnp.cumsum(logf, axis=-1)
    bsz, nh, seq, dh = q.shape
    nb = seq // QBLK
    qb = jnp.moveaxis(q.reshape(bsz, nh, nb, QBLK, dh), 2, 0)
    cb = jnp.moveaxis(c.reshape(bsz, nh, nb, QBLK), 2, 0)
    kpos = jnp.arange(seq)

    def block(inp):
        i, q_i, c_i = inp
        qpos = i * QBLK + jnp.arange(QBLK)
        s = jnp.einsum('bhqd,bhkd->bhqk', q_i, k) + c_i[..., None] - c[:, :, None, :]
        s = jnp.where(kpos[None, :] <= qpos[:, None], s, NEG_BIG)
        return jnp.einsum('bhqk,bhkd->bhqd', jax.nn.softmax(s, axis=-1), v)

    o = lax.map(block, (jnp.arange(nb), qb, cb))
    o = jnp.moveaxis(o, 0, 2).reshape(bsz, nh, seq, dh)
    return _merge(o) * jax.nn.sigmoid(gate.astype(F32))


def _hgrn2(q, f, i, gate, lb, onorm):
    tril, _ = _masks()
    q = jax.nn.silu(_heads(q).astype(F32))
    f = _heads(f).astype(F32)
    v = _heads(i).astype(F32)
    lb = lb.astype(F32).reshape(1, N_HEADS, 1, HEAD_DIM)
    log_fg = jnp.log(lb + (1.0 - lb) * jax.nn.sigmoid(f))
    k = (1.0 - lb) * jax.nn.sigmoid(-f)
    qc, kc, vc = _chunks(q), _chunks(k), _chunks(v)
    bcum = jnp.cumsum(_chunks(log_fg), axis=3)
    mask5 = tril[:, :, None]

    def inner(inp):
        q_i, k_i, v_i, b_i = inp
        dec = _masked_exp(mask5, b_i[:, :, :, None, :] - b_i[:, :, None, :, :])
        scores = jnp.einsum('bhtk,bhtsk,bhsk->bhts', q_i, dec, k_i)
        return jnp.einsum('bhts,bhsv->bhtv', scores, v_i)

    o_in = jnp.moveaxis(lax.map(inner, tuple(jnp.moveaxis(t, 2, 0) for t in (qc, kc, vc, bcum))), 0, 2)
    bend = bcum[..., -1:, :]
    upd = jnp.einsum('bhnsk,bhnsv->bhnkv', kc * jnp.exp(bend - bcum), vc)
    prev = _scan_states(jnp.swapaxes(jnp.exp(bend), -1, -2), upd)
    cross = jnp.einsum('bhntk,bhnkv->bhntv', qc * jnp.exp(bcum), prev)
    o = _unchunk(o_in + cross)
    return _merge(_head_rms(o, onorm)) * jax.nn.silu(gate.astype(F32))


def setup_inputs(seed: int = 0) -> dict:
    key = jax.random.key(seed)
    ks = jax.random.split(key, 20)
    L = DEPTH
    nrm = jax.random.normal
    x = nrm(ks[0], (BATCH, SEQ, D_MODEL), F32)
    norm_mix = 1.0 + 0.02 * nrm(ks[1], (L, D_MODEL), F32)
    norm_ffn = 1.0 + 0.02 * nrm(ks[2], (L, D_MODEL), F32)
    w_in = nrm(ks[3], (L, D_MODEL, PROJ_WIDTH), F32) * D_MODEL ** -0.5
    conv_qkv_a = nrm(ks[4], (L, SHORT_CONV, 3 * GROUP_WIDTH), F32) * SHORT_CONV ** -0.5
    a_log_a = jnp.log(jax.random.uniform(ks[5], (L, N_HEADS), F32, 1.0, 16.0))
    dt = jnp.exp(jax.random.uniform(ks[6], (L, N_HEADS), F32, float(np.log(1e-3)), float(np.log(1e-1))))
    dt_bias_a = dt + jnp.log(-jnp.expm1(-dt))
    onorm_a = 1.0 + 0.02 * nrm(ks[7], (L, HEAD_DIM), F32)
    onorm_b = 1.0 + 0.02 * nrm(ks[8], (L, HEAD_DIM), F32)
    qnorm_c = 1.0 + 0.02 * nrm(ks[9], (L, HEAD_DIM), F32)
    knorm_c = 1.0 + 0.02 * nrm(ks[10], (L, HEAD_DIM), F32)
    fbias_c = jax.random.uniform(ks[11], (L, N_HEADS), F32, 1.0, 4.0)
    lower_bound_d = 0.02 * nrm(ks[12], (L, GROUP_WIDTH), F32)
    onorm_d = 1.0 + 0.02 * nrm(ks[13], (L, HEAD_DIM), F32)
    w_out = nrm(ks[14], (L, MIX_WIDTH, D_MODEL), F32) * MIX_WIDTH ** -0.5
    w_up = nrm(ks[15], (L, D_MODEL, 2 * D_FF), F32) * D_MODEL ** -0.5
    conv_ffn = nrm(ks[16], (L, FFN_CONV, 2 * D_FF), F32) * FFN_CONV ** -0.5
    w_down = nrm(ks[17], (L, D_FF, D_MODEL), F32) * D_FF ** -0.5
    return {'x': x, 'norm_mix': norm_mix, 'norm_ffn': norm_ffn, 'w_in': w_in, 'conv_qkv_a': conv_qkv_a, 'a_log_a': a_log_a, 'dt_bias_a': dt_bias_a, 'onorm_a': onorm_a, 'onorm_b': onorm_b, 'qnorm_c': qnorm_c, 'knorm_c': knorm_c, 'fbias_c': fbias_c, 'lower_bound_d': lower_bound_d, 'onorm_d': onorm_d, 'w_out': w_out, 'w_up': w_up, 'conv_ffn': conv_ffn, 'w_down': w_down}


def reference(x, norm_mix, norm_ffn, w_in, conv_qkv_a, a_log_a, dt_bias_a, onorm_a, onorm_b, qnorm_c, knorm_c, fbias_c, lower_bound_d, onorm_d, w_out, w_up, conv_ffn, w_down):
    seq = x.shape[1]
    inv_freq = ROPE_BASE ** (-jnp.arange(0, HEAD_DIM, 2, dtype=F32) / HEAD_DIM)
    ang = jnp.arange(seq, dtype=F32)[:, None] * inv_freq[None, :]
    cos, sin = jnp.cos(ang), jnp.sin(ang)
    lbs = jax.nn.softmax(lower_bound_d.astype(F32), axis=0)
    lbs = jnp.cumsum(lbs, axis=0) - lbs[0]
    offsets = np.cumsum(PROJ_SIZES)[:-1].tolist()
    h = x
    for l in range(DEPTH):
        u = _rmsnorm(h, norm_mix[l])
        p = u @ w_in[l]
        (qa, ka, va, ga, ba, aa, qb, kb, vb, gb, qc, kc, vc, gc, fc, qd, fd, id_, gd) = jnp.split(p, offsets, axis=-1)
        ya = _gated_deltanet(qa, ka, va, ga, ba, aa, conv_qkv_a[l], a_log_a[l], dt_bias_a[l], onorm_a[l])
        yb = _retention(qb, kb, vb, gb, onorm_b[l], cos, sin)
        yc = _forgetting_attention(qc, kc, vc, gc, fc, qnorm_c[l], knorm_c[l], fbias_c[l])
        yd = _hgrn2(qd, fd, id_, gd, lbs[l], onorm_d[l])
        mix = jnp.concatenate([ya, yb, yc, yd], axis=-1).astype(h.dtype)
        h = h + mix @ w_out[l]
        u = _rmsnorm(h, norm_ffn[l])
        up = _causal_dwconv(u @ w_up[l], conv_ffn[l])
        g_ff, v_ff = jnp.split(up, 2, axis=-1)
        h = h + (jax.nn.silu(g_ff) * v_ff) @ w_down[l]
    return h
```

```python
from functools import partial

import numpy as np
import jax
import jax.numpy as jnp
from jax import lax
from jax.experimental import pallas as pl
from jax.experimental.pallas import tpu as pltpu

F32 = jnp.float32
BF16 = jnp.bfloat16
HI = lax.Precision.HIGHEST

N_GROUPS = 4
HEAD_DIM = 64
N_HEADS = 4
GROUP_WIDTH = N_HEADS * HEAD_DIM
SHORT_CONV = 4
FFN_CONV = 3
ROPE_BASE = 10000.0
RET_DECAY_EXP = 5.0
EPS = 1e-6
NEG_BIG = -1e30
CHUNK = 64
SMALL_W = 128
HALO = 8
VMEM_LIMIT = 56 * 1024 * 1024


def _dot(a, b, precision=None):
    return jnp.dot(a, b, preferred_element_type=F32, precision=precision)


def _dot_nt(a, b, precision=None):
    return lax.dot_general(a, b, (((1,), (1,)), ((), ())), preferred_element_type=F32, precision=precision)


def _dot_tn(a, b, precision=None):
    return lax.dot_general(a, b, (((0,), (0,)), ((), ())), preferred_element_type=F32, precision=precision)


def _sigmoid(x):
    return 1.0 / (1.0 + jnp.exp(-x))


def _silu(x):
    return x * _sigmoid(x)


def _softplus(x):
    return jnp.maximum(x, 0.0) + jnp.log1p(jnp.exp(-jnp.abs(x)))


def _head_slice(h):
    return slice(h * HEAD_DIM, (h + 1) * HEAD_DIM)


def _inproj_kernel(x_ref, g_ref, wm_ref, ws_ref, pm_ref, ps_ref, un_ref):
    @pl.when(pl.program_id(1) == 0)
    def _():
        x = x_ref[...]
        ms = jnp.mean(x * x, axis=-1, keepdims=True)
        un = (x * lax.rsqrt(ms + EPS) * g_ref[...]).astype(BF16)
        un_ref[...] = un
        ps_ref[...] = _dot(un, ws_ref[...])

    pm_ref[...] = _dot(un_ref[...], wm_ref[...]).astype(BF16)


def _inproj(h2d, g, wm, ws, tm=512, tn=1024):
    m, d = h2d.shape
    n = wm.shape[1]
    return pl.pallas_call(
        _inproj_kernel,
        grid=(m // tm, n // tn),
        in_specs=[
            pl.BlockSpec((tm, d), lambda i, j: (i, 0)),
            pl.BlockSpec((1, d), lambda i, j: (0, 0)),
            pl.BlockSpec((d, tn), lambda i, j: (0, j)),
            pl.BlockSpec((d, SMALL_W), lambda i, j: (0, 0)),
        ],
        out_specs=[
            pl.BlockSpec((tm, tn), lambda i, j: (i, j)),
            pl.BlockSpec((tm, SMALL_W), lambda i, j: (i, 0)),
        ],
        out_shape=[
            jax.ShapeDtypeStruct((m, n), BF16),
            jax.ShapeDtypeStruct((m, SMALL_W), F32),
        ],
        scratch_shapes=[pltpu.VMEM((tm, d), BF16)],
        compiler_params=pltpu.CompilerParams(
            dimension_semantics=("parallel", "arbitrary"), vmem_limit_bytes=VMEM_LIMIT),
        name="inproj",
    )(h2d, g, wm, ws)


def _gdn_kernel(p_ref, ps_ref, cw_ref, alog_ref, dtb_ref, on_ref, ltri_ref, utri_ref,
                o_ref, xbuf, q_s, k_s, v_s, la_s, be_s, state):
    t = p_ref.shape[1]
    gw = GROUP_WIDTH

    @pl.when(pl.program_id(1) == 0)
    def _():
        xbuf[0:HALO, :] = jnp.zeros((HALO, 3 * gw), F32)
        state[...] = jnp.zeros_like(state)

    x = p_ref[0, :, 0:3 * gw].astype(F32)
    xbuf[HALO:HALO + t, :] = x
    cw = cw_ref[...]
    y = cw[3:4, :] * x
    for j in range(SHORT_CONV - 1):
        off = HALO - (SHORT_CONV - 1) + j
        y = y + cw[j:j + 1, :] * xbuf[off:off + t, :]
    xbuf[0:HALO, :] = x[t - HALO:t, :]
    y = _silu(y)
    q_s[...] = y[:, 0:gw]
    k_s[...] = y[:, gw:2 * gw]
    v_s[...] = y[:, 2 * gw:3 * gw]

    small = ps_ref[0]
    la_s[...] = -jnp.exp(alog_ref[...]) * _softplus(small + dtb_ref[...])
    be_s[...] = _sigmoid(small)

    row = lax.broadcasted_iota(jnp.int32, (CHUNK, CHUNK), 0)
    col = lax.broadcasted_iota(jnp.int32, (CHUNK, CHUNK), 1)
    tril = row >= col
    strict = row > col
    eye = (row == col).astype(F32)
    ones8 = jnp.ones((8, CHUNK), F32)
    ltri = ltri_ref[...]
    utri = utri_ref[...]
    onorm = on_ref[...]

    def chunk_body(c, carry):
        r = pl.ds(pl.multiple_of(c * CHUNK, CHUNK), CHUNK)
        la_c = la_s[r, :]
        gcol_all = _dot(ltri, la_c, HI)
        be_c = be_s[r, :]
        for h in range(N_HEADS):
            hs = _head_slice(h)
            q = q_s[r, hs]
            k = k_s[r, hs]
            v = v_s[r, hs]
            q = q * lax.rsqrt(jnp.sum(q * q, axis=-1, keepdims=True) + EPS) * HEAD_DIM ** -0.5
            k = k * lax.rsqrt(jnp.sum(k * k, axis=-1, keepdims=True) + EPS)
            beta = be_c[:, h:h + 1]
            gc = gcol_all[:, N_HEADS + h:N_HEADS + h + 1]
            la_col = la_c[:, N_HEADS + h:N_HEADS + h + 1]
            gr = _dot(ones8, la_col * utri, HI)[0:1, :]
            gam = jnp.where(tril, jnp.exp(jnp.where(tril, gc - gr, 0.0)), 0.0)
            kb = k * beta
            a_mat = jnp.where(strict, _dot_nt(kb, k, HI) * gam, 0.0)
            tinv = eye - a_mat
            pw = a_mat
            for _ in range(5):
                pw = _dot(pw, pw, HI)
                tinv = tinv + _dot(tinv, pw, HI)
            eg = jnp.exp(gc)
            u = _dot(tinv, v * beta, HI)
            w = _dot(tinv, kb * eg, HI)
            qk = _dot_nt(q, k, HI) * gam
            gend = gc[CHUNK - 1:CHUNK, :]
            qg = q * eg
            kd = k * jnp.exp(gend - gc)
            s_h = state[h]
            v_new = u - _dot(w, s_h, HI)
            o = _dot(qg, s_h, HI) + _dot(qk, v_new, HI)
            state[h] = jnp.exp(gend) * s_h + _dot_tn(kd, v_new, HI)
            o = o * lax.rsqrt(jnp.mean(o * o, axis=-1, keepdims=True) + EPS) * onorm
            gate = p_ref[0, r, 3 * gw + h * HEAD_DIM:3 * gw + (h + 1) * HEAD_DIM].astype(F32)
            o_ref[0, r, hs] = (o * _silu(gate)).astype(o_ref.dtype)
        return carry

    lax.fori_loop(0, t // CHUNK, chunk_body, 0)


def _gdn(p, ps, cw, alog_row, dtb_row, onorm, ltri, utri, t=512):
    b, s, _ = p.shape
    gw = GROUP_WIDTH
    const = lambda shape: pl.BlockSpec(shape, lambda i, j: (0,) * len(shape))
    return pl.pallas_call(
        _gdn_kernel,
        grid=(b, s // t),
        in_specs=[
            pl.BlockSpec((1, t, 4 * gw), lambda i, j: (i, j, 0)),
            pl.BlockSpec((1, t, SMALL_W), lambda i, j: (i, j, 0)),
            const((SHORT_CONV, 3 * gw)),
            const((1, SMALL_W)),
            const((1, SMALL_W)),
            const((1, HEAD_DIM)),
            const((CHUNK, CHUNK)),
            const((CHUNK, CHUNK)),
        ],
        out_specs=pl.BlockSpec((1, t, gw), lambda i, j: (i, j, 0)),
        out_shape=jax.ShapeDtypeStruct((b, s, gw), BF16),
        scratch_shapes=[
            pltpu.VMEM((t + HALO, 3 * gw), F32),
            pltpu.VMEM((t, gw), F32),
            pltpu.VMEM((t, gw), F32),
            pltpu.VMEM((t, gw), F32),
            pltpu.VMEM((t, SMALL_W), F32),
            pltpu.VMEM((t, SMALL_W), F32),
            pltpu.VMEM((N_HEADS, HEAD_DIM, HEAD_DIM), F32),
        ],
        compiler_params=pltpu.CompilerParams(
            dimension_semantics=("parallel", "arbitrary"), vmem_limit_bytes=VMEM_LIMIT),
        name="gdn",
    )(p, ps, cw, alog_row, dtb_row, onorm, ltri, utri)


def _ret_kernel(p_ref, cos_ref, sin_ref, dmat_ref, wst_ref, wend_ref, gdec_ref, on_ref,
                o_ref, state):
    gw = GROUP_WIDTH

    @pl.when(pl.program_id(1) == 0)
    def _():
        state[...] = jnp.zeros_like(state)

    q = p_ref[0, :, 0:gw].astype(F32)
    k = p_ref[0, :, gw:2 * gw].astype(F32)
    lane = lax.broadcasted_iota(jnp.int32, q.shape, 1)
    first_half = (lane % HEAD_DIM) < (HEAD_DIM // 2)
    cos = cos_ref[...]
    sin = sin_ref[...]

    def rope(x):
        rot = jnp.where(first_half, pltpu.roll(x, gw - HEAD_DIM // 2, 1), pltpu.roll(x, HEAD_DIM // 2, 1))
        return x * cos + rot * sin

    q = rope(q)
    k = rope(k) * HEAD_DIM ** -0.5
    qs = (q * wst_ref[...]).astype(BF16)
    ke = (k * wend_ref[...]).astype(BF16)
    qb = q.astype(BF16)
    kb = k.astype(BF16)
    onorm = on_ref[...]
    for h in range(N_HEADS):
        hs = _head_slice(h)
        v = p_ref[0, :, 2 * gw + h * HEAD_DIM:2 * gw + (h + 1) * HEAD_DIM]
        scores = (_dot_nt(qb[:, hs], kb[:, hs]) * dmat_ref[h]).astype(BF16)
        s_h = state[h]
        o = _dot(scores, v) + _dot(qs[:, hs], s_h.astype(BF16))
        state[h] = gdec_ref[h] * s_h + _dot_tn(ke[:, hs], v)
        oc = o - jnp.mean(o, axis=-1, keepdims=True)
        o = oc * lax.rsqrt(jnp.mean(oc * oc, axis=-1, keepdims=True) + EPS) * onorm
        gate = p_ref[0, :, 3 * gw + h * HEAD_DIM:3 * gw + (h + 1) * HEAD_DIM].astype(F32)
        o_ref[0, :, hs] = (o * _silu(gate)).astype(o_ref.dtype)


def _ret(p, cos, sin, dmat, wst, wend, gdec, onorm, t):
    b, s, _ = p.shape
    gw = GROUP_WIDTH
    return pl.pallas_call(
        _ret_kernel,
        grid=(b, s // t),
        in_specs=[
            pl.BlockSpec((1, t, 4 * gw), lambda i, j: (i, j, 1)),
            pl.BlockSpec((t, gw), lambda i, j: (j, 0)),
            pl.BlockSpec((t, gw), lambda i, j: (j, 0)),
            pl.BlockSpec((N_HEADS, t, t), lambda i, j: (0, 0, 0)),
            pl.BlockSpec((t, gw), lambda i, j: (0, 0)),
            pl.BlockSpec((t, gw), lambda i, j: (0, 0)),
            pl.BlockSpec(memory_space=pltpu.SMEM),
            pl.BlockSpec((1, HEAD_DIM), lambda i, j: (0, 0)),
        ],
        out_specs=pl.BlockSpec((1, t, gw), lambda i, j: (i, j, 0)),
        out_shape=jax.ShapeDtypeStruct((b, s, gw), BF16),
        scratch_shapes=[pltpu.VMEM((N_HEADS, HEAD_DIM, HEAD_DIM), F32)],
        compiler_params=pltpu.CompilerParams(
            dimension_semantics=("parallel", "arbitrary"), vmem_limit_bytes=VMEM_LIMIT),
        name="retention",
    )(p, cos, sin, dmat, wst, wend, gdec, onorm)


def _fox_prep_kernel(p_ref, ps_ref, qn_ref, kn_ref, fb_ref, utri_ref,
                     q_out, k_out, v_out, c_out, carry):
    gw = GROUP_WIDTH

    @pl.when(pl.program_id(1) == 0)
    def _():
        carry[...] = jnp.zeros_like(carry)

    qn = qn_ref[...]
    kn = kn_ref[...]
    for h in range(N_HEADS):
        q = p_ref[0, :, h * HEAD_DIM:(h + 1) * HEAD_DIM].astype(F32)
        k = p_ref[0, :, gw + h * HEAD_DIM:gw + (h + 1) * HEAD_DIM].astype(F32)
        q = q * lax.rsqrt(jnp.mean(q * q, axis=-1, keepdims=True) + EPS) * qn * HEAD_DIM ** -0.5
        k = k * lax.rsqrt(jnp.mean(k * k, axis=-1, keepdims=True) + EPS) * kn
        q_out[0, h] = q.astype(BF16)
        k_out[0, h] = k.astype(BF16)
        v_out[0, h] = p_ref[0, :, 2 * gw + h * HEAD_DIM:2 * gw + (h + 1) * HEAD_DIM]

    x = ps_ref[0] + fb_ref[...]
    logf = jnp.minimum(x, 0.0) - jnp.log1p(jnp.exp(-jnp.abs(x)))
    logf_t = logf.T[8:16, :]
    c = _dot(logf_t, utri_ref[...], HI) + carry[:, 0:1]
    c_out[0] = c
    t = c.shape[1]
    carry[...] = jnp.broadcast_to(c[:, t - 1:t], carry.shape)


def _fox_prep(p, ps, qn, kn, fb_row, utri, t):
    b, s, _ = p.shape
    gw = GROUP_WIDTH
    hm = jax.ShapeDtypeStruct((b, N_HEADS, s, HEAD_DIM), BF16)
    hm_spec = pl.BlockSpec((1, N_HEADS, t, HEAD_DIM), lambda i, j: (i, 0, j, 0))
    return pl.pallas_call(
        _fox_prep_kernel,
        grid=(b, s // t),
        in_specs=[
            pl.BlockSpec((1, t, 4 * gw), lambda i, j: (i, j, 2)),
            pl.BlockSpec((1, t, SMALL_W), lambda i, j: (i, j, 0)),
            pl.BlockSpec((1, HEAD_DIM), lambda i, j: (0, 0)),
            pl.BlockSpec((1, HEAD_DIM), lambda i, j: (0, 0)),
            pl.BlockSpec((1, SMALL_W), lambda i, j: (0, 0)),
            pl.BlockSpec((t, t), lambda i, j: (0, 0)),
        ],
        out_specs=[hm_spec, hm_spec, hm_spec, pl.BlockSpec((1, 8, t), lambda i, j: (i, 0, j))],
        out_shape=[hm, hm, hm, jax.ShapeDtypeStruct((b, 8, s), F32)],
        scratch_shapes=[pltpu.VMEM((8, 128), F32)],
        compiler_params=pltpu.CompilerParams(
            dimension_semantics=("parallel", "arbitrary"), vmem_limit_bytes=VMEM_LIMIT),
        name="fox_prep",
    )(p, ps, qn, kn, fb_row, utri)


def _fox_kernel(q_ref, k_ref, v_ref, c_ref, g_ref, o_ref, m_s, l_s, acc_s):
    qi = pl.program_id(1)
    ki = pl.program_id(2)
    tq = q_ref.shape[2]
    tk = k_ref.shape[2]

    @pl.when(ki == 0)
    def _():
        m_s[...] = jnp.full_like(m_s, NEG_BIG)
        l_s[...] = jnp.zeros_like(l_s)
        acc_s[...] = jnp.zeros_like(acc_s)

    def update(masked):
        if masked:
            row = lax.broadcasted_iota(jnp.int32, (tq, tk), 0)
            col = lax.broadcasted_iota(jnp.int32, (tq, tk), 1)
            keep = row >= col
        for h in range(N_HEADS):
            s = _dot_nt(q_ref[0, h], k_ref[0, h]) - c_ref[0, h:h + 1, :]
            if masked:
                s = jnp.where(keep, s, NEG_BIG)
            m_old = m_s[h]
            m_new = jnp.maximum(m_old, jnp.max(s, axis=-1, keepdims=True))
            alpha = jnp.exp(m_old - m_new)
            p = jnp.exp(s - m_new)
            l_s[h] = alpha * l_s[h] + jnp.sum(p, axis=-1, keepdims=True)
            acc_s[h] = alpha * acc_s[h] + _dot(p.astype(BF16), v_ref[0, h])
            m_s[h] = m_new

    @pl.when(ki < qi)
    def _():
        update(False)

    @pl.when(ki == qi)
    def _():
        update(True)
        for h in range(N_HEADS):
            hs = _head_slice(h)
            o = acc_s[h] / l_s[h]
            gate = g_ref[0, :, hs].astype(F32)
            o_ref[0, :, hs] = (o * _sigmoid(gate)).astype(o_ref.dtype)


def _fox(qh, kh, vh, c, p, t):
    b, _, s, _ = qh.shape
    gw = GROUP_WIDTH
    n = s // t
    kv_spec = pl.BlockSpec((1, N_HEADS, t, HEAD_DIM), lambda i, j, k: (i, 0, jnp.minimum(k, j), 0))
    return pl.pallas_call(
        _fox_kernel,
        grid=(b, n, n),
        in_specs=[
            pl.BlockSpec((1, N_HEADS, t, HEAD_DIM), lambda i, j, k: (i, 0, j, 0)),
            kv_spec,
            kv_spec,
            pl.BlockSpec((1, 8, t), lambda i, j, k: (i, 0, jnp.minimum(k, j))),
            pl.BlockSpec((1, t, gw), lambda i, j, k: (i, j, 4 * 2 + 3)),
        ],
        out_specs=pl.BlockSpec((1, t, gw), lambda i, j, k: (i, j, 0)),
        out_shape=jax.ShapeDtypeStruct((b, s, gw), BF16),
        scratch_shapes=[
            pltpu.VMEM((N_HEADS, t, 1), F32),
            pltpu.VMEM((N_HEADS, t, 1), F32),
            pltpu.VMEM((N_HEADS, t, HEAD_DIM), F32),
        ],
        compiler_params=pltpu.CompilerParams(
            dimension_semantics=("parallel", "parallel", "arbitrary"), vmem_limit_bytes=VMEM_LIMIT),
        name="fox_attn",
    )(qh, kh, vh, c, p)


N_LEVELS = 6


def _hgrn_tables():
    c = CHUNK
    mall = np.zeros((N_LEVELS + 2, c, c), np.float32)
    masks = np.zeros((N_LEVELS + 1, c, c), np.float32)
    for lv in range(N_LEVELS):
        half = 1 << lv
        for t in range(c):
            m = (t >> (lv + 1)) * (2 * half) + half
            if t >= m:
                mall[lv, t, m:t + 1] = 1.0
            else:
                mall[lv, t, t + 1:m] = 1.0
            for s in range(c):
                if (s >> (lv + 1)) == (t >> (lv + 1)) and t >= m and s < m:
                    masks[lv, t, s] = 1.0
    masks[N_LEVELS] = np.eye(c, dtype=np.float32)
    for t in range(c):
        mall[N_LEVELS, t, :t + 1] = 1.0
        mall[N_LEVELS + 1, t, t + 1:] = 1.0
    return mall.reshape((N_LEVELS + 2) * c, c), masks


def _hgrn_kernel(p_ref, lb_ref, on_ref, mall_ref, mask_ref, o_ref, state):
    t = p_ref.shape[1]
    gw = GROUP_WIDTH

    @pl.when(pl.program_id(1) == 0)
    def _():
        state[...] = jnp.zeros_like(state)

    lb = lb_ref[...]
    onorm = on_ref[...]
    mall = mall_ref[...]

    def chunk_body(c, carry):
        r = pl.ds(pl.multiple_of(c * CHUNK, CHUNK), CHUNK)
        qx = p_ref[0, r, 0:gw].astype(F32)
        f = p_ref[0, r, gw:2 * gw].astype(F32)
        logf = jnp.log(lb + (1.0 - lb) * _sigmoid(f))
        kk = (1.0 - lb) * _sigmoid(-f)
        qq = _silu(qx)
        x_all = jnp.exp(_dot(mall, logf, HI))
        for h in range(N_HEADS):
            hs = _head_slice(h)
            q = qq[:, hs]
            k = kk[:, hs]
            v = p_ref[0, r, 2 * gw + h * HEAD_DIM:2 * gw + (h + 1) * HEAD_DIM].astype(F32)
            scores = _dot_nt(q, k, HI) * mask_ref[N_LEVELS]
            for lv in range(N_LEVELS):
                x_l = x_all[lv * CHUNK:(lv + 1) * CHUNK, hs]
                scores = scores + _dot_nt(q * x_l, k * x_l, HI) * mask_ref[lv]
            x_q = x_all[N_LEVELS * CHUNK:(N_LEVELS + 1) * CHUNK, hs]
            x_k = x_all[(N_LEVELS + 1) * CHUNK:(N_LEVELS + 2) * CHUNK, hs]
            st = state[h]
            o = _dot(scores, v, HI) + _dot_nt(q * x_q, st, HI)
            state[h] = st * x_q[CHUNK - 1:CHUNK, :] + _dot_tn(v, k * x_k, HI)
            o = o * lax.rsqrt(jnp.mean(o * o, axis=-1, keepdims=True) + EPS) * onorm
            gate = p_ref[0, r, 3 * gw + h * HEAD_DIM:3 * gw + (h + 1) * HEAD_DIM].astype(F32)
            o_ref[0, r, hs] = (o * _silu(gate)).astype(o_ref.dtype)
        return carry

    lax.fori_loop(0, t // CHUNK, chunk_body, 0)


def _hgrn(p, lb_row, onorm, mall, masks, t=512):
    b, s, _ = p.shape
    gw = GROUP_WIDTH
    return pl.pallas_call(
        _hgrn_kernel,
        grid=(b, s // t),
        in_specs=[
            pl.BlockSpec((1, t, 4 * gw), lambda i, j: (i, j, 3)),
            pl.BlockSpec((1, gw), lambda i, j: (0, 0)),
            pl.BlockSpec((1, HEAD_DIM), lambda i, j: (0, 0)),
            pl.BlockSpec(mall.shape, lambda i, j: (0, 0)),
            pl.BlockSpec(masks.shape, lambda i, j: (0, 0, 0)),
        ],
        out_specs=pl.BlockSpec((1, t, gw), lambda i, j: (i, j, 0)),
        out_shape=jax.ShapeDtypeStruct((b, s, gw), BF16),
        scratch_shapes=[pltpu.VMEM((N_HEADS, HEAD_DIM, HEAD_DIM), F32)],
        compiler_params=pltpu.CompilerParams(
            dimension_semantics=("parallel", "arbitrary"), vmem_limit_bytes=VMEM_LIMIT),
        name="hgrn2",
    )(p, lb_row, onorm, mall, masks)


def _post_kernel(h_ref, ya_ref, yb_ref, yc_ref, yd_ref, wo_ref, g_ref, wg_ref, wv_ref,
                 cg_ref, cv_ref, wd_ref, o_ref, carry_g, carry_v, buf_g, buf_v, un_s, *, tf):
    tm = h_ref.shape[1]
    gw = GROUP_WIDTH
    d_ff = wg_ref.shape[1]

    @pl.when(pl.program_id(1) == 0)
    def _():
        carry_g[...] = jnp.zeros_like(carry_g)
        carry_v[...] = jnp.zeros_like(carry_v)

    h1 = h_ref[0]
    for g, y_ref in enumerate((ya_ref, yb_ref, yc_ref, yd_ref)):
        h1 = h1 + _dot(y_ref[0], wo_ref[g * gw:(g + 1) * gw, :])
    ms = jnp.mean(h1 * h1, axis=-1, keepdims=True)
    un_s[...] = (h1 * lax.rsqrt(ms + EPS) * g_ref[...]).astype(BF16)
    o_ref[0] = h1

    def conv(up, carry, buf, cw, fs):
        buf[0:HALO, :] = carry[:, fs]
        buf[HALO:HALO + tm, :] = up
        carry[:, fs] = up[tm - HALO:tm, :]
        out = cw[FFN_CONV - 1:FFN_CONV, fs] * up
        for j in range(FFN_CONV - 1):
            off = HALO - (FFN_CONV - 1) + j
            out = out + cw[j:j + 1, fs] * buf[off:off + tm, :]
        return out

    for f in range(d_ff // tf):
        fs = slice(f * tf, (f + 1) * tf)
        cg = conv(_dot(un_s[...], wg_ref[:, fs]), carry_g, buf_g, cg_ref, fs)
        cv = conv(_dot(un_s[...], wv_ref[:, fs]), carry_v, buf_v, cv_ref, fs)
        act = (_silu(cg) * cv).astype(BF16)
        o_ref[0] += _dot(act, wd_ref[fs, :])


def _post(h, ya, yb, yc, yd, wo, g, wg, wv, cg, cv, wd, tm=512, tf=256):
    b, s, d = h.shape
    gw = GROUP_WIDTH
    d_ff = wg.shape[1]
    const = lambda shape: pl.BlockSpec(shape, lambda i, j: (0,) * len(shape))
    y_spec = pl.BlockSpec((1, tm, gw), lambda i, j: (i, j, 0))
    return pl.pallas_call(
        partial(_post_kernel, tf=tf),
        grid=(b, s // tm),
        in_specs=[
            pl.BlockSpec((1, tm, d), lambda i, j: (i, j, 0)),
            y_spec, y_spec, y_spec, y_spec,
            const((N_GROUPS * gw, d)),
            const((1, d)),
            const((d, d_ff)),
            const((d, d_ff)),
            const((FFN_CONV, d_ff)),
            const((FFN_CONV, d_ff)),
            const((d_ff, d)),
        ],
        out_specs=pl.BlockSpec((1, tm, d), lambda i, j: (i, j, 0)),
        out_shape=jax.ShapeDtypeStruct((b, s, d), F32),
        scratch_shapes=[
            pltpu.VMEM((HALO, d_ff), F32),
            pltpu.VMEM((HALO, d_ff), F32),
            pltpu.VMEM((tm + HALO, tf), F32),
            pltpu.VMEM((tm + HALO, tf), F32),
            pltpu.VMEM((tm, d), BF16),
        ],
        compiler_params=pltpu.CompilerParams(
            dimension_semantics=("parallel", "arbitrary"), vmem_limit_bytes=VMEM_LIMIT),
        name="post_ffn",
    )(h, ya, yb, yc, yd, wo, g, wg, wv, cg, cv, wd)


def _retention_tables(seq, t):
    hd = HEAD_DIM
    inv_freq = ROPE_BASE ** (-jnp.arange(0, hd, 2, dtype=F32) / hd)
    ang = jnp.arange(seq, dtype=F32)[:, None] * inv_freq[None, :]
    cos, sin = jnp.cos(ang), jnp.sin(ang)
    cos_t = jnp.tile(jnp.concatenate([cos, cos], axis=-1), (1, N_HEADS))
    sin_t = jnp.tile(jnp.concatenate([-sin, sin], axis=-1), (1, N_HEADS))
    lgh = jnp.log1p(-jnp.exp2(-RET_DECAY_EXP - jnp.arange(N_HEADS, dtype=F32)))
    n = jnp.arange(t, dtype=F32)
    diff = n[:, None] - n[None, :]
    keep = diff >= 0
    dmat = jnp.where(keep[None], jnp.exp(jnp.where(keep, diff, 0.0)[None] * lgh[:, None, None]), 0.0)
    wst = jnp.repeat(jnp.exp((n + 1.0)[:, None] * lgh[None, :]), hd, axis=1)
    wend = jnp.repeat(jnp.exp((t - 1.0 - n)[:, None] * lgh[None, :]), hd, axis=1)
    gdec = jnp.exp(t * lgh)
    return cos_t, sin_t, dmat, wst, wend, gdec


def kernel(x, norm_mix, norm_ffn, w_in, conv_qkv_a, a_log_a, dt_bias_a, onorm_a, onorm_b, qnorm_c, knorm_c,
           fbias_c, lower_bound_d, onorm_d, w_out, w_up, conv_ffn, w_down):
    b, s, d = x.shape
    depth = w_in.shape[0]
    gw = GROUP_WIDTH
    nh = N_HEADS
    d_ff = w_down.shape[1]
    t_ret = 256
    t_fox = 512

    oa = 4 * gw
    ob = oa + 2 * nh
    oc = ob + 4 * gw
    od = oc + 4 * gw + nh
    w_main = jnp.concatenate(
        [w_in[:, :, 0:oa], w_in[:, :, ob:ob + 4 * gw], w_in[:, :, oc:oc + 4 * gw], w_in[:, :, od:od + 4 * gw]],
        axis=-1).astype(BF16)
    w_small = jnp.concatenate([w_in[:, :, oa:ob], w_in[:, :, oc + 4 * gw:od]], axis=-1)
    w_small = jnp.pad(w_small, ((0, 0), (0, 0), (0, SMALL_W - 3 * nh))).astype(BF16)
    w_out_b = w_out.astype(BF16)
    w_g = w_up[:, :, :d_ff].astype(BF16)
    w_v = w_up[:, :, d_ff:].astype(BF16)
    w_down_b = w_down.astype(BF16)

    def small_row(vals, off):
        return jnp.zeros((depth, 1, SMALL_W), F32).at[:, 0, off:off + nh].set(vals.astype(F32))

    alog_rows = small_row(a_log_a, nh)
    dtb_rows = small_row(dt_bias_a, nh)
    fb_rows = small_row(fbias_c, 2 * nh)

    lbs = jax.nn.softmax(lower_bound_d.astype(F32), axis=0)
    lbs = jnp.cumsum(lbs, axis=0) - lbs[0]

    ltri = jnp.asarray(np.tril(np.ones((CHUNK, CHUNK), np.float32)))
    utri = jnp.asarray(np.triu(np.ones((CHUNK, CHUNK), np.float32)))
    utri_fox = jnp.asarray(np.triu(np.ones((t_fox, t_fox), np.float32)))
    mall_np, masks_np = _hgrn_tables()
    mall = jnp.asarray(mall_np)
    masks = jnp.asarray(masks_np)
    cos_t, sin_t, dmat, wst, wend, gdec = _retention_tables(s, t_ret)

    h = x.astype(F32)
    for l in range(depth):
        pm, ps = _inproj(h.reshape(b * s, d), norm_mix[l].reshape(1, d).astype(F32), w_main[l], w_small[l])
        pm = pm.reshape(b, s, 4 * 4 * gw)
        ps = ps.reshape(b, s, SMALL_W)
        ya = _gdn(pm, ps, conv_qkv_a[l].astype(F32), alog_rows[l], dtb_rows[l],
                  onorm_a[l].reshape(1, HEAD_DIM).astype(F32), ltri, utri)
        yb = _ret(pm, cos_t, sin_t, dmat, wst, wend, gdec, onorm_b[l].reshape(1, HEAD_DIM).astype(F32), t_ret)
        qh, kh, vh, c = _fox_prep(pm, ps, qnorm_c[l].reshape(1, HEAD_DIM).astype(F32),
                                  knorm_c[l].reshape(1, HEAD_DIM).astype(F32), fb_rows[l], utri_fox, t_fox)
        yc = _fox(qh, kh, vh, c, pm, t_fox)
        yd = _hgrn(pm, lbs[l].reshape(1, gw), onorm_d[l].reshape(1, HEAD_DIM).astype(F32), mall, masks)
        h = _post(h, ya, yb, yc, yd, w_out_b[l], norm_ffn[l].reshape(1, d).astype(F32), w_g[l], w_v[l],
                  conv_ffn[l][:, :d_ff].astype(F32), conv_ffn[l][:, d_ff:].astype(F32), w_down_b[l])
    return h.astype(x.dtype)
```

```python
from functools import partial

import numpy as np
import jax
import jax.numpy as jnp
from jax import lax
from jax.experimental import pallas as pl
from jax.experimental.pallas import tpu as pltpu

F32 = jnp.float32
BF16 = jnp.bfloat16
HI = lax.Precision.HIGHEST

N_GROUPS = 4
HEAD_DIM = 64
N_HEADS = 4
GROUP_WIDTH = N_HEADS * HEAD_DIM
SHORT_CONV = 4
FFN_CONV = 3
ROPE_BASE = 10000.0
RET_DECAY_EXP = 5.0
EPS = 1e-6
NEG_BIG = -1e30
CHUNK = 64
SMALL_W = 128
HALO = 8
VMEM_LIMIT = 56 * 1024 * 1024


def _dot(a, b, precision=None):
    return jnp.dot(a, b, preferred_element_type=F32, precision=precision)


def _dot_nt(a, b, precision=None):
    return lax.dot_general(a, b, (((1,), (1,)), ((), ())), preferred_element_type=F32, precision=precision)


def _dot_tn(a, b, precision=None):
    return lax.dot_general(a, b, (((0,), (0,)), ((), ())), preferred_element_type=F32, precision=precision)


def _sigmoid(x):
    return 1.0 / (1.0 + jnp.exp(-x))


def _silu(x):
    return x * _sigmoid(x)


def _softplus(x):
    return jnp.maximum(x, 0.0) + jnp.log1p(jnp.exp(-jnp.abs(x)))


def _head_slice(h):
    return slice(h * HEAD_DIM, (h + 1) * HEAD_DIM)


def _inproj_kernel(x_ref, g_ref, wm_ref, ws_ref, pm_ref, ps_ref, un_ref):
    @pl.when(pl.program_id(1) == 0)
    def _():
        x = x_ref[...]
        ms = jnp.mean(x * x, axis=-1, keepdims=True)
        un = (x * lax.rsqrt(ms + EPS) * g_ref[...]).astype(BF16)
        un_ref[...] = un
        ps_ref[...] = _dot(un, ws_ref[...])

    pm_ref[...] = _dot(un_ref[...], wm_ref[...]).astype(BF16)


def _inproj(h2d, g, wm, ws, tm=1024, tn=1024):
    m, d = h2d.shape
    n = wm.shape[1]
    return pl.pallas_call(
        _inproj_kernel,
        grid=(m // tm, n // tn),
        in_specs=[
            pl.BlockSpec((tm, d), lambda i, j: (i, 0)),
            pl.BlockSpec((1, d), lambda i, j: (0, 0)),
            pl.BlockSpec((d, tn), lambda i, j: (0, j)),
            pl.BlockSpec((d, SMALL_W), lambda i, j: (0, 0)),
        ],
        out_specs=[
            pl.BlockSpec((tm, tn), lambda i, j: (i, j)),
            pl.BlockSpec((tm, SMALL_W), lambda i, j: (i, 0)),
        ],
        out_shape=[
            jax.ShapeDtypeStruct((m, n), BF16),
            jax.ShapeDtypeStruct((m, SMALL_W), F32),
        ],
        scratch_shapes=[pltpu.VMEM((tm, d), BF16)],
        compiler_params=pltpu.CompilerParams(
            dimension_semantics=("parallel", "arbitrary"), vmem_limit_bytes=VMEM_LIMIT),
        name="inproj",
    )(h2d, g, wm, ws)


def _dot01(m, x):
    hi = x.astype(BF16)
    lo = (x - hi.astype(F32)).astype(BF16)
    return _dot(m, hi) + _dot(m, lo)


def _bdot(a, b):
    return lax.dot_general(a, b, (((2,), (1,)), ((0,), (0,))), preferred_element_type=F32)


def _bdot_nt(a, b):
    return lax.dot_general(a, b, (((2,), (2,)), ((0,), (0,))), preferred_element_type=F32)


GDN_GROUP = 2


def _dot01_right(x, m):
    hi = x.astype(BF16)
    lo = (x - hi.astype(F32)).astype(BF16)
    return _dot(hi, m) + _dot(lo, m)


def _gdn_kernel(p_ref, ps_ref, cw_ref, alog_ref, dtb_ref, on_ref, ltri_ref, utri_ref,
                o_ref, xbuf, q_s, k_s, v_s, la_s, be_s, g_s, u_s, w_s, qk_s, qg_s, kd_s, state):
    t = p_ref.shape[1]
    gw = GROUP_WIDTH

    @pl.when(pl.program_id(1) == 0)
    def _():
        xbuf[0:HALO, :] = jnp.zeros((HALO, 3 * gw), F32)
        state[...] = jnp.zeros_like(state)

    x = p_ref[0, :, 0:3 * gw].astype(F32)
    xbuf[HALO:HALO + t, :] = x
    cw = cw_ref[...]
    y = cw[3:4, :] * x
    for j in range(SHORT_CONV - 1):
        off = HALO - (SHORT_CONV - 1) + j
        y = y + cw[j:j + 1, :] * xbuf[off:off + t, :]
    xbuf[0:HALO, :] = x[t - HALO:t, :]
    y = _silu(y)
    q_s[...] = y[:, 0:gw]
    k_s[...] = y[:, gw:2 * gw]
    v_s[...] = y[:, 2 * gw:3 * gw]

    small = ps_ref[0]
    la_s[...] = -jnp.exp(alog_ref[...]) * _softplus(small + dtb_ref[...])
    be_s[...] = _sigmoid(small)

    row = lax.broadcasted_iota(jnp.int32, (CHUNK, CHUNK), 0)
    col = lax.broadcasted_iota(jnp.int32, (CHUNK, CHUNK), 1)
    tril = row >= col
    strict = row > col
    eye = (row == col).astype(F32)
    ones8 = jnp.ones((8, CHUNK), BF16)
    ltri = ltri_ref[...]
    utri = utri_ref[...].astype(F32)

    def factor_body(ci, carry):
        rows, qs, ks, vs, betas, gcs, grs = [], [], [], [], [], [], []
        for g in range(GDN_GROUP):
            r = pl.ds(pl.multiple_of((ci * GDN_GROUP + g) * CHUNK, CHUNK), CHUNK)
            rows.append(r)
            la_c = la_s[r, :]
            gcol_all = _dot01(ltri, la_c)
            g_s[r, :] = gcol_all
            be_c = be_s[r, :]
            for h in range(N_HEADS):
                hs = _head_slice(h)
                q = q_s[r, hs]
                k = k_s[r, hs]
                qs.append(q * lax.rsqrt(jnp.sum(q * q, axis=-1, keepdims=True) + EPS) * HEAD_DIM ** -0.5)
                ks.append(k * lax.rsqrt(jnp.sum(k * k, axis=-1, keepdims=True) + EPS))
                vs.append(v_s[r, hs])
                betas.append(be_c[:, h:h + 1])
                gcs.append(gcol_all[:, N_HEADS + h:N_HEADS + h + 1])
                la_col = la_c[:, N_HEADS + h:N_HEADS + h + 1]
                grs.append(_dot01(ones8, la_col * utri)[0:1, :])
        q = jnp.stack(qs)
        k = jnp.stack(ks)
        v = jnp.stack(vs)
        beta = jnp.stack(betas)
        gc = jnp.stack(gcs)
        gr = jnp.stack(grs)
        gam = jnp.where(tril, jnp.exp(jnp.where(tril, gc - gr, 0.0)), 0.0)
        kb = k * beta
        kbf = k.astype(BF16)
        a_mat = jnp.where(strict, _bdot_nt(kb.astype(BF16), kbf) * gam, 0.0)
        tinv = eye - a_mat
        pw = a_mat
        for _ in range(5):
            pwb = pw.astype(BF16)
            pw = _bdot(pwb, pwb)
            tinv = tinv + _bdot(tinv.astype(BF16), pw.astype(BF16))
        tb = tinv.astype(BF16)
        eg = jnp.exp(gc)
        gend = gc[:, CHUNK - 1:CHUNK, :]
        u = _bdot(tb, (v * beta).astype(BF16))
        w = _bdot(tb, (kb * eg).astype(BF16)).astype(BF16)
        qk = (_bdot_nt(q.astype(BF16), kbf) * gam).astype(BF16)
        qg = (q * eg).astype(BF16)
        kd = (k * jnp.exp(gend - gc)).astype(BF16)
        for g in range(GDN_GROUP):
            for h in range(N_HEADS):
                i = g * N_HEADS + h
                hs = _head_slice(h)
                u_s[rows[g], hs] = u[i]
                w_s[rows[g], hs] = w[i]
                qk_s[rows[g], hs] = qk[i]
                qg_s[rows[g], hs] = qg[i]
                kd_s[rows[g], hs] = kd[i]
        return carry

    lax.fori_loop(0, t // (CHUNK * GDN_GROUP), factor_body, 0)

    brow = lax.broadcasted_iota(jnp.int32, (gw, gw), 0) // HEAD_DIM
    bcol = lax.broadcasted_iota(jnp.int32, (gw, gw), 1) // HEAD_DIM
    same_head = brow == bcol
    head_ones = same_head.astype(BF16)
    lane_head = lax.broadcasted_iota(jnp.int32, (1, gw), 1) // HEAD_DIM
    onorm = on_ref[...]

    def scan_body(c, carry):
        r = pl.ds(pl.multiple_of(c * CHUNK, CHUNK), CHUNK)
        s_f = state[...]
        s_b = s_f.astype(BF16)
        v_new = u_s[r, :] - _dot(w_s[r, :], s_b)
        vnb = v_new.astype(BF16)
        v_bd = jnp.where(same_head, jnp.concatenate([vnb] * N_HEADS, axis=0), jnp.zeros((), BF16))
        o = _dot(qg_s[r, :], s_b) + _dot(qk_s[r, :], v_bd)
        gend = g_s[r, :][CHUNK - 1:CHUNK, :]
        ge_row = jnp.zeros((1, gw), F32)
        for h in range(N_HEADS):
            ge_row = jnp.where(lane_head == h, jnp.exp(gend[:, N_HEADS + h:N_HEADS + h + 1]), ge_row)
        state[...] = s_f * ge_row + jnp.where(same_head, _dot_tn(kd_s[r, :], vnb), 0.0)
        ms = _dot01_right(o * o, head_ones) * (1.0 / HEAD_DIM)
        gate = p_ref[0, r, 3 * gw:4 * gw].astype(F32)
        o_ref[0, r, :] = (o * lax.rsqrt(ms + EPS) * onorm * _silu(gate)).astype(o_ref.dtype)
        return carry

    lax.fori_loop(0, t // CHUNK, scan_body, 0)


def _gdn(p, ps, cw, alog_row, dtb_row, onorm_t, ltri, utri, t=512):
    b, s, _ = p.shape
    gw = GROUP_WIDTH
    const = lambda shape: pl.BlockSpec(shape, lambda i, j: (0,) * len(shape))
    return pl.pallas_call(
        _gdn_kernel,
        grid=(b, s // t),
        in_specs=[
            pl.BlockSpec((1, t, 4 * gw), lambda i, j: (i, j, 0)),
            pl.BlockSpec((1, t, SMALL_W), lambda i, j: (i, j, 0)),
            const((SHORT_CONV, 3 * gw)),
            const((1, SMALL_W)),
            const((1, SMALL_W)),
            const((1, gw)),
            const((CHUNK, CHUNK)),
            const((CHUNK, CHUNK)),
        ],
        out_specs=pl.BlockSpec((1, t, gw), lambda i, j: (i, j, 0)),
        out_shape=jax.ShapeDtypeStruct((b, s, gw), BF16),
        scratch_shapes=[
            pltpu.VMEM((t + HALO, 3 * gw), F32),
            pltpu.VMEM((t, gw), F32),
            pltpu.VMEM((t, gw), F32),
            pltpu.VMEM((t, gw), F32),
            pltpu.VMEM((t, SMALL_W), F32),
            pltpu.VMEM((t, SMALL_W), F32),
            pltpu.VMEM((t, SMALL_W), F32),
            pltpu.VMEM((t, gw), F32),
            pltpu.VMEM((t, gw), BF16),
            pltpu.VMEM((t, gw), BF16),
            pltpu.VMEM((t, gw), BF16),
            pltpu.VMEM((t, gw), BF16),
            pltpu.VMEM((gw, gw), F32),
        ],
        compiler_params=pltpu.CompilerParams(
            dimension_semantics=("parallel", "arbitrary"), vmem_limit_bytes=VMEM_LIMIT),
        name="gdn",
    )(p, ps, cw, alog_row, dtb_row, onorm_t, ltri, utri)


def _ret_kernel(p_ref, cos_ref, sin_ref, dmat_ref, wst_ref, wend_ref, gdec_ref, on_ref,
                o_ref, state):
    gw = GROUP_WIDTH

    @pl.when(pl.program_id(1) == 0)
    def _():
        state[...] = jnp.zeros_like(state)

    q = p_ref[0, :, 0:gw].astype(F32)
    k = p_ref[0, :, gw:2 * gw].astype(F32)
    lane = lax.broadcasted_iota(jnp.int32, q.shape, 1)
    first_half = (lane % HEAD_DIM) < (HEAD_DIM // 2)
    cos = cos_ref[...]
    sin = sin_ref[...]

    def rope(x):
        rot = jnp.where(first_half, pltpu.roll(x, gw - HEAD_DIM // 2, 1), pltpu.roll(x, HEAD_DIM // 2, 1))
        return x * cos + rot * sin

    q = rope(q)
    k = rope(k) * HEAD_DIM ** -0.5
    qs = (q * wst_ref[...]).astype(BF16)
    ke = (k * wend_ref[...]).astype(BF16)
    qb = q.astype(BF16)
    kb = k.astype(BF16)
    onorm = on_ref[...]
    for h in range(N_HEADS):
        hs = _head_slice(h)
        v = p_ref[0, :, 2 * gw + h * HEAD_DIM:2 * gw + (h + 1) * HEAD_DIM]
        scores = (_dot_nt(qb[:, hs], kb[:, hs]) * dmat_ref[h]).astype(BF16)
        s_h = state[h]
        o = _dot(scores, v) + _dot(qs[:, hs], s_h.astype(BF16))
        state[h] = gdec_ref[h] * s_h + _dot_tn(ke[:, hs], v)
        oc = o - jnp.mean(o, axis=-1, keepdims=True)
        o = oc * lax.rsqrt(jnp.mean(oc * oc, axis=-1, keepdims=True) + EPS) * onorm
        gate = p_ref[0, :, 3 * gw + h * HEAD_DIM:3 * gw + (h + 1) * HEAD_DIM].astype(F32)
        o_ref[0, :, hs] = (o * _silu(gate)).astype(o_ref.dtype)


def _ret(p, cos, sin, dmat, wst, wend, gdec, onorm, t):
    b, s, _ = p.shape
    gw = GROUP_WIDTH
    return pl.pallas_call(
        _ret_kernel,
        grid=(b, s // t),
        in_specs=[
            pl.BlockSpec((1, t, 4 * gw), lambda i, j: (i, j, 1)),
            pl.BlockSpec((t, gw), lambda i, j: (j, 0)),
            pl.BlockSpec((t, gw), lambda i, j: (j, 0)),
            pl.BlockSpec((N_HEADS, t, t), lambda i, j: (0, 0, 0)),
            pl.BlockSpec((t, gw), lambda i, j: (0, 0)),
            pl.BlockSpec((t, gw), lambda i, j: (0, 0)),
            pl.BlockSpec(memory_space=pltpu.SMEM),
            pl.BlockSpec((1, HEAD_DIM), lambda i, j: (0, 0)),
        ],
        out_specs=pl.BlockSpec((1, t, gw), lambda i, j: (i, j, 0)),
        out_shape=jax.ShapeDtypeStruct((b, s, gw), BF16),
        scratch_shapes=[pltpu.VMEM((N_HEADS, HEAD_DIM, HEAD_DIM), F32)],
        compiler_params=pltpu.CompilerParams(
            dimension_semantics=("parallel", "arbitrary"), vmem_limit_bytes=VMEM_LIMIT),
        name="retention",
    )(p, cos, sin, dmat, wst, wend, gdec, onorm)


def _fox_prep_kernel(p_ref, ps_ref, qn_ref, kn_ref, fb_ref, utri_ref,
                     q_out, k_out, v_out, c_out, carry):
    gw = GROUP_WIDTH

    @pl.when(pl.program_id(1) == 0)
    def _():
        carry[...] = jnp.zeros_like(carry)

    qn = qn_ref[...]
    kn = kn_ref[...]
    ones_col = (lax.broadcasted_iota(jnp.int32, (p_ref.shape[1], HEAD_DIM), 1) == 0).astype(BF16)
    for h in range(N_HEADS):
        q = p_ref[0, :, h * HEAD_DIM:(h + 1) * HEAD_DIM].astype(F32)
        k = p_ref[0, :, gw + h * HEAD_DIM:gw + (h + 1) * HEAD_DIM].astype(F32)
        q = q * lax.rsqrt(jnp.mean(q * q, axis=-1, keepdims=True) + EPS) * qn * HEAD_DIM ** -0.5
        k = k * lax.rsqrt(jnp.mean(k * k, axis=-1, keepdims=True) + EPS) * kn
        q_out[0, h] = q.astype(BF16)
        k_out[0, h] = k.astype(BF16)
        v_out[0, h, :, 0:HEAD_DIM] = p_ref[0, :, 2 * gw + h * HEAD_DIM:2 * gw + (h + 1) * HEAD_DIM]
        v_out[0, h, :, HEAD_DIM:2 * HEAD_DIM] = ones_col

    x = ps_ref[0] + fb_ref[...]
    logf = jnp.minimum(x, 0.0) - jnp.log1p(jnp.exp(-jnp.abs(x)))
    logf_t = logf.T[8:16, :]
    c = _dot(logf_t, utri_ref[...], HI) + carry[:, 0:1]
    c_out[0] = c
    t = c.shape[1]
    carry[...] = jnp.broadcast_to(c[:, t - 1:t], carry.shape)


def _fox_prep(p, ps, qn, kn, fb_row, utri, t):
    b, s, _ = p.shape
    gw = GROUP_WIDTH
    hm = jax.ShapeDtypeStruct((b, N_HEADS, s, HEAD_DIM), BF16)
    hm_spec = pl.BlockSpec((1, N_HEADS, t, HEAD_DIM), lambda i, j: (i, 0, j, 0))
    hv = jax.ShapeDtypeStruct((b, N_HEADS, s, 2 * HEAD_DIM), BF16)
    hv_spec = pl.BlockSpec((1, N_HEADS, t, 2 * HEAD_DIM), lambda i, j: (i, 0, j, 0))
    return pl.pallas_call(
        _fox_prep_kernel,
        grid=(b, s // t),
        in_specs=[
            pl.BlockSpec((1, t, 4 * gw), lambda i, j: (i, j, 2)),
            pl.BlockSpec((1, t, SMALL_W), lambda i, j: (i, j, 0)),
            pl.BlockSpec((1, HEAD_DIM), lambda i, j: (0, 0)),
            pl.BlockSpec((1, HEAD_DIM), lambda i, j: (0, 0)),
            pl.BlockSpec((1, SMALL_W), lambda i, j: (0, 0)),
            pl.BlockSpec((t, t), lambda i, j: (0, 0)),
        ],
        out_specs=[hm_spec, hm_spec, hv_spec, pl.BlockSpec((1, 8, t), lambda i, j: (i, 0, j))],
        out_shape=[hm, hm, hv, jax.ShapeDtypeStruct((b, 8, s), F32)],
        scratch_shapes=[pltpu.VMEM((8, 128), F32)],
        compiler_params=pltpu.CompilerParams(
            dimension_semantics=("parallel", "arbitrary"), vmem_limit_bytes=VMEM_LIMIT),
        name="fox_prep",
    )(p, ps, qn, kn, fb_row, utri)


FOX_STRIP = 32
LANES = 128


def _fox_kernel(q_ref, k_ref, v_ref, c_ref, g_ref, o_ref, m_s, acc_s, s_scr, p_scr, al_scr):
    qi = pl.program_id(1)
    ki = pl.program_id(2)
    tq = q_ref.shape[2]
    tk = k_ref.shape[2]
    nj = tk // LANES

    @pl.when(ki == 0)
    def _():
        m_s[...] = jnp.full_like(m_s, NEG_BIG)
        acc_s[...] = jnp.zeros_like(acc_s)

    def update(masked):
        for h in range(N_HEADS):
            s_scr[h] = _dot_nt(q_ref[0, h], k_ref[0, h])
        row = lax.broadcasted_iota(jnp.int32, (FOX_STRIP, LANES), 0)
        col = lax.broadcasted_iota(jnp.int32, (FOX_STRIP, LANES), 1)
        for h in range(N_HEADS):
            c_blk = [c_ref[0, h:h + 1, j * LANES:(j + 1) * LANES] for j in range(nj)]
            for i in range(tq // FOX_STRIP):
                r0 = i * FOX_STRIP
                r = slice(r0, r0 + FOX_STRIP)
                live = min(nj, (r0 + FOX_STRIP - 1) // LANES + 1) if masked else nj
                sb = [s_scr[h, r, j * LANES:(j + 1) * LANES] - c_blk[j] for j in range(live)]
                if masked:
                    sb = [jnp.where(row + r0 >= col + j * LANES, sb[j], NEG_BIG)
                          if (j + 1) * LANES - 1 > r0 else sb[j] for j in range(live)]
                mx = sb[0]
                for j in range(1, live):
                    mx = jnp.maximum(mx, sb[j])
                m_old = m_s[h, r, :]
                m_new = jnp.maximum(m_old, jnp.max(mx, axis=-1, keepdims=True))
                m_s[h, r, :] = m_new
                al_scr[h, r, :] = jnp.exp(m_old - m_new)
                for j in range(live):
                    p_scr[h, r, j * LANES:(j + 1) * LANES] = jnp.exp(sb[j] - m_new).astype(BF16)
                for j in range(live, nj):
                    p_scr[h, r, j * LANES:(j + 1) * LANES] = jnp.zeros((FOX_STRIP, LANES), BF16)
        for h in range(N_HEADS):
            acc_s[h] = al_scr[h] * acc_s[h] + _dot(p_scr[h], v_ref[0, h])

    @pl.when(ki < qi)
    def _():
        update(False)

    @pl.when(ki == qi)
    def _():
        update(True)
        for h in range(N_HEADS):
            hs = _head_slice(h)
            acc = acc_s[h]
            o = acc[:, 0:HEAD_DIM] / acc[:, HEAD_DIM:HEAD_DIM + 1]
            gate = g_ref[0, :, hs].astype(F32)
            o_ref[0, :, hs] = (o * _sigmoid(gate)).astype(o_ref.dtype)


def _fox(qh, kh, vh, c, p, t):
    b, _, s, _ = qh.shape
    gw = GROUP_WIDTH
    n = s // t
    kidx = lambda i, j, k: (i, 0, jnp.minimum(k, j), 0)
    return pl.pallas_call(
        _fox_kernel,
        grid=(b, n, n),
        in_specs=[
            pl.BlockSpec((1, N_HEADS, t, HEAD_DIM), lambda i, j, k: (i, 0, j, 0)),
            pl.BlockSpec((1, N_HEADS, t, HEAD_DIM), kidx),
            pl.BlockSpec((1, N_HEADS, t, 2 * HEAD_DIM), kidx),
            pl.BlockSpec((1, 8, t), lambda i, j, k: (i, 0, jnp.minimum(k, j))),
            pl.BlockSpec((1, t, gw), lambda i, j, k: (i, j, 4 * 2 + 3)),
        ],
        out_specs=pl.BlockSpec((1, t, gw), lambda i, j, k: (i, j, 0)),
        out_shape=jax.ShapeDtypeStruct((b, s, gw), BF16),
        scratch_shapes=[
            pltpu.VMEM((N_HEADS, t, LANES), F32),
            pltpu.VMEM((N_HEADS, t, 2 * HEAD_DIM), F32),
            pltpu.VMEM((N_HEADS, t, t), F32),
            pltpu.VMEM((N_HEADS, t, t), BF16),
            pltpu.VMEM((N_HEADS, t, LANES), F32),
        ],
        compiler_params=pltpu.CompilerParams(
            dimension_semantics=("parallel", "parallel", "arbitrary"), vmem_limit_bytes=VMEM_LIMIT),
        name="fox_attn",
    )(qh, kh, vh, c, p)


N_LEVELS = 6


def _hgrn_tables():
    c = CHUNK
    mall = np.zeros((N_LEVELS + 2, c, c), np.float32)
    masks = np.zeros((N_LEVELS + 1, c, c), np.float32)
    for lv in range(N_LEVELS):
        half = 1 << lv
        for t in range(c):
            m = (t >> (lv + 1)) * (2 * half) + half
            if t >= m:
                mall[lv, t, m:t + 1] = 1.0
            else:
                mall[lv, t, t + 1:m] = 1.0
            for s in range(c):
                if (s >> (lv + 1)) == (t >> (lv + 1)) and t >= m and s < m:
                    masks[lv, t, s] = 1.0
    masks[N_LEVELS] = np.eye(c, dtype=np.float32)
    for t in range(c):
        mall[N_LEVELS, t, :t + 1] = 1.0
        mall[N_LEVELS + 1, t, t + 1:] = 1.0
    return mall.reshape((N_LEVELS + 2) * c, c), masks


def _hgrn_kernel(p_ref, lb_ref, on_ref, mall_ref, mask_ref, o_ref, state):
    t = p_ref.shape[1]
    gw = GROUP_WIDTH

    @pl.when(pl.program_id(1) == 0)
    def _():
        state[...] = jnp.zeros_like(state)

    lb = lb_ref[...]
    onorm = on_ref[...]
    mall = mall_ref[...]

    def chunk_body(c, carry):
        r = pl.ds(pl.multiple_of(c * CHUNK, CHUNK), CHUNK)
        qx = p_ref[0, r, 0:gw].astype(F32)
        f = p_ref[0, r, gw:2 * gw].astype(F32)
        logf = jnp.log(lb + (1.0 - lb) * _sigmoid(f))
        kk = (1.0 - lb) * _sigmoid(-f)
        qq = _silu(qx)
        x_all = jnp.exp(_dot(mall, logf.astype(BF16)))
        for h in range(N_HEADS):
            hs = _head_slice(h)
            q = qq[:, hs]
            k = kk[:, hs]
            v = p_ref[0, r, 2 * gw + h * HEAD_DIM:2 * gw + (h + 1) * HEAD_DIM]
            scores = _dot_nt(q.astype(BF16), k.astype(BF16)) * mask_ref[N_LEVELS]
            for lv in range(N_LEVELS):
                x_l = x_all[lv * CHUNK:(lv + 1) * CHUNK, hs]
                scores = scores + _dot_nt((q * x_l).astype(BF16), (k * x_l).astype(BF16)) * mask_ref[lv]
            x_q = x_all[N_LEVELS * CHUNK:(N_LEVELS + 1) * CHUNK, hs]
            x_k = x_all[(N_LEVELS + 1) * CHUNK:(N_LEVELS + 2) * CHUNK, hs]
            st = state[h]
            o = _dot(scores.astype(BF16), v) + _dot_nt((q * x_q).astype(BF16), st.astype(BF16))
            state[h] = st * x_q[CHUNK - 1:CHUNK, :] + _dot_tn(v, (k * x_k).astype(BF16))
            o = o * lax.rsqrt(jnp.mean(o * o, axis=-1, keepdims=True) + EPS) * onorm
            gate = p_ref[0, r, 3 * gw + h * HEAD_DIM:3 * gw + (h + 1) * HEAD_DIM].astype(F32)
            o_ref[0, r, hs] = (o * _silu(gate)).astype(o_ref.dtype)
        return carry

    lax.fori_loop(0, t // CHUNK, chunk_body, 0, unroll=2)


def _hgrn(p, lb_row, onorm, mall, masks, t=512):
    b, s, _ = p.shape
    gw = GROUP_WIDTH
    return pl.pallas_call(
        _hgrn_kernel,
        grid=(b, s // t),
        in_specs=[
            pl.BlockSpec((1, t, 4 * gw), lambda i, j: (i, j, 3)),
            pl.BlockSpec((1, gw), lambda i, j: (0, 0)),
            pl.BlockSpec((1, HEAD_DIM), lambda i, j: (0, 0)),
            pl.BlockSpec(mall.shape, lambda i, j: (0, 0)),
            pl.BlockSpec(masks.shape, lambda i, j: (0, 0, 0)),
        ],
        out_specs=pl.BlockSpec((1, t, gw), lambda i, j: (i, j, 0)),
        out_shape=jax.ShapeDtypeStruct((b, s, gw), BF16),
        scratch_shapes=[pltpu.VMEM((N_HEADS, HEAD_DIM, HEAD_DIM), F32)],
        compiler_params=pltpu.CompilerParams(
            dimension_semantics=("parallel", "arbitrary"), vmem_limit_bytes=VMEM_LIMIT),
        name="hgrn2",
    )(p, lb_row, onorm, mall, masks)


def _post_kernel(h_ref, ya_ref, yb_ref, yc_ref, yd_ref, wo_ref, g_ref, wg_ref, wv_ref,
                 cg_ref, cv_ref, wd_ref, o_ref, carry_g, carry_v, buf_g, buf_v, un_s, *, tf):
    tm = h_ref.shape[1]
    gw = GROUP_WIDTH
    d_ff = wg_ref.shape[1]

    @pl.when(pl.program_id(1) == 0)
    def _():
        carry_g[...] = jnp.zeros_like(carry_g)
        carry_v[...] = jnp.zeros_like(carry_v)

    h1 = h_ref[0]
    for g, y_ref in enumerate((ya_ref, yb_ref, yc_ref, yd_ref)):
        h1 = h1 + _dot(y_ref[0], wo_ref[g * gw:(g + 1) * gw, :])
    ms = jnp.mean(h1 * h1, axis=-1, keepdims=True)
    un_s[...] = (h1 * lax.rsqrt(ms + EPS) * g_ref[...]).astype(BF16)
    o_ref[0] = h1

    def conv(up, carry, buf, cw, fs):
        buf[0:HALO, :] = carry[:, fs]
        buf[HALO:HALO + tm, :] = up
        carry[:, fs] = up[tm - HALO:tm, :]
        out = cw[FFN_CONV - 1:FFN_CONV, fs] * up
        for j in range(FFN_CONV - 1):
            off = HALO - (FFN_CONV - 1) + j
            out = out + cw[j:j + 1, fs] * buf[off:off + tm, :]
        return out

    for f in range(d_ff // tf):
        fs = slice(f * tf, (f + 1) * tf)
        cg = conv(_dot(un_s[...], wg_ref[:, fs]), carry_g, buf_g, cg_ref, fs)
        cv = conv(_dot(un_s[...], wv_ref[:, fs]), carry_v, buf_v, cv_ref, fs)
        act = (_silu(cg) * cv).astype(BF16)
        o_ref[0] += _dot(act, wd_ref[fs, :])


def _post(h, ya, yb, yc, yd, wo, g, wg, wv, cg, cv, wd, tm=512, tf=256):
    b, s, d = h.shape
    gw = GROUP_WIDTH
    d_ff = wg.shape[1]
    const = lambda shape: pl.BlockSpec(shape, lambda i, j: (0,) * len(shape))
    y_spec = pl.BlockSpec((1, tm, gw), lambda i, j: (i, j, 0))
    return pl.pallas_call(
        partial(_post_kernel, tf=tf),
        grid=(b, s // tm),
        in_specs=[
            pl.BlockSpec((1, tm, d), lambda i, j: (i, j, 0)),
            y_spec, y_spec, y_spec, y_spec,
            const((N_GROUPS * gw, d)),
            const((1, d)),
            const((d, d_ff)),
            const((d, d_ff)),
            const((FFN_CONV, d_ff)),
            const((FFN_CONV, d_ff)),
            const((d_ff, d)),
        ],
        out_specs=pl.BlockSpec((1, tm, d), lambda i, j: (i, j, 0)),
        out_shape=jax.ShapeDtypeStruct((b, s, d), F32),
        scratch_shapes=[
            pltpu.VMEM((HALO, d_ff), F32),
            pltpu.VMEM((HALO, d_ff), F32),
            pltpu.VMEM((tm + HALO, tf), F32),
            pltpu.VMEM((tm + HALO, tf), F32),
            pltpu.VMEM((tm, d), BF16),
        ],
        compiler_params=pltpu.CompilerParams(
            dimension_semantics=("parallel", "arbitrary"), vmem_limit_bytes=VMEM_LIMIT),
        name="post_ffn",
    )(h, ya, yb, yc, yd, wo, g, wg, wv, cg, cv, wd)


def _retention_tables(seq, t):
    hd = HEAD_DIM
    inv_freq = ROPE_BASE ** (-jnp.arange(0, hd, 2, dtype=F32) / hd)
    ang = jnp.arange(seq, dtype=F32)[:, None] * inv_freq[None, :]
    cos, sin = jnp.cos(ang), jnp.sin(ang)
    cos_t = jnp.tile(jnp.concatenate([cos, cos], axis=-1), (1, N_HEADS))
    sin_t = jnp.tile(jnp.concatenate([-sin, sin], axis=-1), (1, N_HEADS))
    lgh = jnp.log1p(-jnp.exp2(-RET_DECAY_EXP - jnp.arange(N_HEADS, dtype=F32)))
    n = jnp.arange(t, dtype=F32)
    diff = n[:, None] - n[None, :]
    keep = diff >= 0
    dmat = jnp.where(keep[None], jnp.exp(jnp.where(keep, diff, 0.0)[None] * lgh[:, None, None]), 0.0)
    wst = jnp.repeat(jnp.exp((n + 1.0)[:, None] * lgh[None, :]), hd, axis=1)
    wend = jnp.repeat(jnp.exp((t - 1.0 - n)[:, None] * lgh[None, :]), hd, axis=1)
    gdec = jnp.exp(t * lgh)
    return cos_t, sin_t, dmat, wst, wend, gdec


def kernel(x, norm_mix, norm_ffn, w_in, conv_qkv_a, a_log_a, dt_bias_a, onorm_a, onorm_b, qnorm_c, knorm_c,
           fbias_c, lower_bound_d, onorm_d, w_out, w_up, conv_ffn, w_down):
    b, s, d = x.shape
    depth = w_in.shape[0]
    gw = GROUP_WIDTH
    nh = N_HEADS
    d_ff = w_down.shape[1]
    t_ret = 256
    t_fox = 512

    oa = 4 * gw
    ob = oa + 2 * nh
    oc = ob + 4 * gw
    od = oc + 4 * gw + nh
    w_main = jnp.concatenate(
        [w_in[:, :, 0:oa], w_in[:, :, ob:ob + 4 * gw], w_in[:, :, oc:oc + 4 * gw], w_in[:, :, od:od + 4 * gw]],
        axis=-1).astype(BF16)
    w_small = jnp.concatenate([w_in[:, :, oa:ob], w_in[:, :, oc + 4 * gw:od]], axis=-1)
    w_small = jnp.pad(w_small, ((0, 0), (0, 0), (0, SMALL_W - 3 * nh))).astype(BF16)
    w_out_b = w_out.astype(BF16)
    w_g = w_up[:, :, :d_ff].astype(BF16)
    w_v = w_up[:, :, d_ff:].astype(BF16)
    w_down_b = w_down.astype(BF16)

    def small_row(vals, off):
        return jnp.zeros((depth, 1, SMALL_W), F32).at[:, 0, off:off + nh].set(vals.astype(F32))

    alog_rows = small_row(a_log_a, nh)
    dtb_rows = small_row(dt_bias_a, nh)
    fb_rows = small_row(fbias_c, 2 * nh)

    lbs = jax.nn.softmax(lower_bound_d.astype(F32), axis=0)
    lbs = jnp.cumsum(lbs, axis=0) - lbs[0]

    ltri = jnp.asarray(np.tril(np.ones((CHUNK, CHUNK), np.float32))).astype(BF16)
    utri = jnp.asarray(np.triu(np.ones((CHUNK, CHUNK), np.float32))).astype(BF16)
    utri_fox = jnp.asarray(np.triu(np.ones((t_fox, t_fox), np.float32)))
    mall_np, masks_np = _hgrn_tables()
    mall = jnp.asarray(mall_np).astype(BF16)
    masks = jnp.asarray(masks_np)
    cos_t, sin_t, dmat, wst, wend, gdec = _retention_tables(s, t_ret)

    h = x.astype(F32)
    for l in range(depth):
        pm, ps = _inproj(h.reshape(b * s, d), norm_mix[l].reshape(1, d).astype(F32), w_main[l], w_small[l])
        pm = pm.reshape(b, s, 4 * 4 * gw)
        ps = ps.reshape(b, s, SMALL_W)
        ya = _gdn(pm, ps, conv_qkv_a[l].astype(F32), alog_rows[l], dtb_rows[l],
                  jnp.tile(onorm_a[l].reshape(1, HEAD_DIM).astype(F32), (1, nh)), ltri, utri)
        yb = _ret(pm, cos_t, sin_t, dmat, wst, wend, gdec, onorm_b[l].reshape(1, HEAD_DIM).astype(F32), t_ret)
        qh, kh, vh, c = _fox_prep(pm, ps, qnorm_c[l].reshape(1, HEAD_DIM).astype(F32),
                                  knorm_c[l].reshape(1, HEAD_DIM).astype(F32), fb_rows[l], utri_fox, t_fox)
        yc = _fox(qh, kh, vh, c, pm, t_fox)
        yd = _hgrn(pm, lbs[l].reshape(1, gw), onorm_d[l].reshape(1, HEAD_DIM).astype(F32), mall, masks)
        h = _post(h, ya, yb, yc, yd, w_out_b[l], norm_ffn[l].reshape(1, d).astype(F32), w_g[l], w_v[l],
                  conv_ffn[l][:, :d_ff].astype(F32), conv_ffn[l][:, d_ff:].astype(F32), w_down_b[l])
    return h.astype(x.dtype)
```

```python
from functools import partial

import numpy as np
import jax
import jax.numpy as jnp
from jax import lax
from jax.experimental import pallas as pl
from jax.experimental.pallas import tpu as pltpu

F32 = jnp.float32
BF16 = jnp.bfloat16
HI = lax.Precision.HIGHEST

N_GROUPS = 4
HEAD_DIM = 64
N_HEADS = 4
GROUP_WIDTH = N_HEADS * HEAD_DIM
SHORT_CONV = 4
FFN_CONV = 3
ROPE_BASE = 10000.0
RET_DECAY_EXP = 5.0
EPS = 1e-6
NEG_BIG = -1e30
CHUNK = 64
SMALL_W = 128
HALO = 8
VMEM_LIMIT = 56 * 1024 * 1024


def _dot(a, b, precision=None):
    return jnp.dot(a, b, preferred_element_type=F32, precision=precision)


def _dot_nt(a, b, precision=None):
    return lax.dot_general(a, b, (((1,), (1,)), ((), ())), preferred_element_type=F32, precision=precision)


def _dot_tn(a, b, precision=None):
    return lax.dot_general(a, b, (((0,), (0,)), ((), ())), preferred_element_type=F32, precision=precision)


def _sigmoid(x):
    return 1.0 / (1.0 + jnp.exp(-x))


def _silu(x):
    return x * _sigmoid(x)


def _softplus(x):
    return jnp.maximum(x, 0.0) + jnp.log1p(jnp.exp(-jnp.abs(x)))


def _head_slice(h):
    return slice(h * HEAD_DIM, (h + 1) * HEAD_DIM)


def _inproj_kernel(x_ref, g_ref, wm_ref, ws_ref, pm_ref, ps_ref, un_ref):
    @pl.when(pl.program_id(1) == 0)
    def _():
        x = x_ref[...]
        ms = jnp.mean(x * x, axis=-1, keepdims=True)
        un = (x * lax.rsqrt(ms + EPS) * g_ref[...]).astype(BF16)
        un_ref[...] = un
        ps_ref[...] = _dot(un, ws_ref[...])

    pm_ref[...] = _dot(un_ref[...], wm_ref[...]).astype(BF16)


def _inproj(h2d, g, wm, ws, tm=1024, tn=1024):
    m, d = h2d.shape
    n = wm.shape[1]
    return pl.pallas_call(
        _inproj_kernel,
        grid=(m // tm, n // tn),
        in_specs=[
            pl.BlockSpec((tm, d), lambda i, j: (i, 0)),
            pl.BlockSpec((1, d), lambda i, j: (0, 0)),
            pl.BlockSpec((d, tn), lambda i, j: (0, j)),
            pl.BlockSpec((d, SMALL_W), lambda i, j: (0, 0)),
        ],
        out_specs=[
            pl.BlockSpec((tm, tn), lambda i, j: (i, j)),
            pl.BlockSpec((tm, SMALL_W), lambda i, j: (i, 0)),
        ],
        out_shape=[
            jax.ShapeDtypeStruct((m, n), BF16),
            jax.ShapeDtypeStruct((m, SMALL_W), F32),
        ],
        scratch_shapes=[pltpu.VMEM((tm, d), BF16)],
        compiler_params=pltpu.CompilerParams(
            dimension_semantics=("parallel", "arbitrary"), vmem_limit_bytes=VMEM_LIMIT),
        name="inproj",
    )(h2d, g, wm, ws)


def _dot01(m, x):
    hi = x.astype(BF16)
    lo = (x - hi.astype(F32)).astype(BF16)
    return _dot(m, hi) + _dot(m, lo)


def _bdot(a, b):
    return lax.dot_general(a, b, (((2,), (1,)), ((0,), (0,))), preferred_element_type=F32)


def _bdot_nt(a, b):
    return lax.dot_general(a, b, (((2,), (2,)), ((0,), (0,))), preferred_element_type=F32)


GDN_GROUP = 4


def _dot01_right(x, m):
    hi = x.astype(BF16)
    lo = (x - hi.astype(F32)).astype(BF16)
    return _dot(hi, m) + _dot(lo, m)


def _gdn_kernel(p_ref, ps_ref, cw_ref, alog_ref, dtb_ref, on_ref, ltri_ref, utri_ref,
                o_ref, xbuf, q_s, k_s, v_s, la_s, be_s, g_s, u_s, w_s, qk_s, qg_s, kd_s, state):
    t = p_ref.shape[1]
    gw = GROUP_WIDTH

    @pl.when(pl.program_id(1) == 0)
    def _():
        xbuf[0:HALO, :] = jnp.zeros((HALO, 3 * gw), F32)
        state[...] = jnp.zeros_like(state)

    x = p_ref[0, :, 0:3 * gw].astype(F32)
    xbuf[HALO:HALO + t, :] = x
    cw = cw_ref[...]
    y = cw[3:4, :] * x
    for j in range(SHORT_CONV - 1):
        off = HALO - (SHORT_CONV - 1) + j
        y = y + cw[j:j + 1, :] * xbuf[off:off + t, :]
    xbuf[0:HALO, :] = x[t - HALO:t, :]
    y = _silu(y)
    brow = lax.broadcasted_iota(jnp.int32, (gw, gw), 0) // HEAD_DIM
    bcol = lax.broadcasted_iota(jnp.int32, (gw, gw), 1) // HEAD_DIM
    same_head = brow == bcol
    head_ones = same_head.astype(BF16)
    q = y[:, 0:gw]
    k = y[:, gw:2 * gw]
    q_s[...] = q * lax.rsqrt(_dot01_right(q * q, head_ones) + EPS) * HEAD_DIM ** -0.5
    k_s[...] = k * lax.rsqrt(_dot01_right(k * k, head_ones) + EPS)
    v_s[...] = y[:, 2 * gw:3 * gw]

    small = ps_ref[0]
    la_s[...] = -jnp.exp(alog_ref[...]) * _softplus(small + dtb_ref[...])
    be_s[...] = _sigmoid(small)

    row = lax.broadcasted_iota(jnp.int32, (CHUNK, CHUNK), 0)
    col = lax.broadcasted_iota(jnp.int32, (CHUNK, CHUNK), 1)
    tril = row >= col
    strict = row > col
    eye = (row == col).astype(F32)
    ones8 = jnp.ones((8, CHUNK), BF16)
    ltri = ltri_ref[...]
    utri = utri_ref[...].astype(F32)

    def factor_body(ci, carry):
        rows, qs, ks, vs, betas, gcs, grs = [], [], [], [], [], [], []
        for g in range(GDN_GROUP):
            r = pl.ds(pl.multiple_of((ci * GDN_GROUP + g) * CHUNK, CHUNK), CHUNK)
            rows.append(r)
            la_c = la_s[r, :]
            gcol_all = _dot01(ltri, la_c)
            g_s[r, :] = gcol_all
            be_c = be_s[r, :]
            for h in range(N_HEADS):
                hs = _head_slice(h)
                qs.append(q_s[r, hs])
                ks.append(k_s[r, hs])
                vs.append(v_s[r, hs])
                betas.append(be_c[:, h:h + 1])
                gcs.append(gcol_all[:, N_HEADS + h:N_HEADS + h + 1])
                la_col = la_c[:, N_HEADS + h:N_HEADS + h + 1]
                grs.append(_dot01(ones8, la_col * utri)[0:1, :])
        q = jnp.stack(qs)
        k = jnp.stack(ks)
        v = jnp.stack(vs)
        beta = jnp.stack(betas)
        gc = jnp.stack(gcs)
        gr = jnp.stack(grs)
        gam = jnp.where(tril, jnp.exp(jnp.where(tril, gc - gr, 0.0)), 0.0)
        kb = k * beta
        kbf = k.astype(BF16)
        a_mat = jnp.where(strict, _bdot_nt(kb.astype(BF16), kbf) * gam, 0.0)
        tinv = eye - a_mat
        pw = a_mat
        for _ in range(5):
            pwb = pw.astype(BF16)
            pw = _bdot(pwb, pwb)
            tinv = tinv + _bdot(tinv.astype(BF16), pw.astype(BF16))
        tb = tinv.astype(BF16)
        eg = jnp.exp(gc)
        gend = gc[:, CHUNK - 1:CHUNK, :]
        u = _bdot(tb, (v * beta).astype(BF16))
        w = _bdot(tb, (kb * eg).astype(BF16)).astype(BF16)
        qk = (_bdot_nt(q.astype(BF16), kbf) * gam).astype(BF16)
        qg = (q * eg).astype(BF16)
        kd = (k * jnp.exp(gend - gc)).astype(BF16)
        for g in range(GDN_GROUP):
            for h in range(N_HEADS):
                i = g * N_HEADS + h
                hs = _head_slice(h)
                u_s[rows[g], hs] = u[i]
                w_s[rows[g], hs] = w[i]
                qk_s[rows[g], hs] = qk[i]
                qg_s[rows[g], hs] = qg[i]
                kd_s[rows[g], hs] = kd[i]
        return carry

    lax.fori_loop(0, t // (CHUNK * GDN_GROUP), factor_body, 0)

    lane_head = lax.broadcasted_iota(jnp.int32, (1, gw), 1) // HEAD_DIM
    onorm = on_ref[...]

    def scan_body(c, carry):
        r = pl.ds(pl.multiple_of(c * CHUNK, CHUNK), CHUNK)
        s_f = state[...]
        s_b = s_f.astype(BF16)
        v_new = u_s[r, :] - _dot(w_s[r, :], s_b)
        vnb = v_new.astype(BF16)
        v_bd = jnp.where(same_head, jnp.concatenate([vnb] * N_HEADS, axis=0), jnp.zeros((), BF16))
        o = _dot(qg_s[r, :], s_b) + _dot(qk_s[r, :], v_bd)
        gend = g_s[r, :][CHUNK - 1:CHUNK, :]
        ge_row = jnp.zeros((1, gw), F32)
        for h in range(N_HEADS):
            ge_row = jnp.where(lane_head == h, jnp.exp(gend[:, N_HEADS + h:N_HEADS + h + 1]), ge_row)
        state[...] = s_f * ge_row + jnp.where(same_head, _dot_tn(kd_s[r, :], vnb), 0.0)
        ms = _dot01_right(o * o, head_ones) * (1.0 / HEAD_DIM)
        gate = p_ref[0, r, 3 * gw:4 * gw].astype(F32)
        o_ref[0, r, :] = (o * lax.rsqrt(ms + EPS) * onorm * _silu(gate)).astype(o_ref.dtype)
        return carry

    lax.fori_loop(0, t // CHUNK, scan_body, 0)


def _gdn(p, ps, cw, alog_row, dtb_row, onorm_t, ltri, utri, t=512):
    b, s, _ = p.shape
    gw = GROUP_WIDTH
    const = lambda shape: pl.BlockSpec(shape, lambda i, j: (0,) * len(shape))
    return pl.pallas_call(
        _gdn_kernel,
        grid=(b, s // t),
        in_specs=[
            pl.BlockSpec((1, t, 4 * gw), lambda i, j: (i, j, 0)),
            pl.BlockSpec((1, t, SMALL_W), lambda i, j: (i, j, 0)),
            const((SHORT_CONV, 3 * gw)),
            const((1, SMALL_W)),
            const((1, SMALL_W)),
            const((1, gw)),
            const((CHUNK, CHUNK)),
            const((CHUNK, CHUNK)),
        ],
        out_specs=pl.BlockSpec((1, t, gw), lambda i, j: (i, j, 0)),
        out_shape=jax.ShapeDtypeStruct((b, s, gw), BF16),
        scratch_shapes=[
            pltpu.VMEM((t + HALO, 3 * gw), F32),
            pltpu.VMEM((t, gw), F32),
            pltpu.VMEM((t, gw), F32),
            pltpu.VMEM((t, gw), F32),
            pltpu.VMEM((t, SMALL_W), F32),
            pltpu.VMEM((t, SMALL_W), F32),
            pltpu.VMEM((t, SMALL_W), F32),
            pltpu.VMEM((t, gw), F32),
            pltpu.VMEM((t, gw), BF16),
            pltpu.VMEM((t, gw), BF16),
            pltpu.VMEM((t, gw), BF16),
            pltpu.VMEM((t, gw), BF16),
            pltpu.VMEM((gw, gw), F32),
        ],
        compiler_params=pltpu.CompilerParams(
            dimension_semantics=("parallel", "arbitrary"), vmem_limit_bytes=VMEM_LIMIT),
        name="gdn",
    )(p, ps, cw, alog_row, dtb_row, onorm_t, ltri, utri)


def _ret_kernel(p_ref, cos_ref, sin_ref, dmat_ref, wst_ref, wend_ref, gdec_ref, on_ref,
                o_ref, state):
    gw = GROUP_WIDTH

    @pl.when(pl.program_id(1) == 0)
    def _():
        state[...] = jnp.zeros_like(state)

    q = p_ref[0, :, 0:gw].astype(F32)
    k = p_ref[0, :, gw:2 * gw].astype(F32)
    lane = lax.broadcasted_iota(jnp.int32, q.shape, 1)
    first_half = (lane % HEAD_DIM) < (HEAD_DIM // 2)
    cos = cos_ref[...]
    sin = sin_ref[...]

    def rope(x):
        rot = jnp.where(first_half, pltpu.roll(x, gw - HEAD_DIM // 2, 1), pltpu.roll(x, HEAD_DIM // 2, 1))
        return x * cos + rot * sin

    q = rope(q)
    k = rope(k) * HEAD_DIM ** -0.5
    qs = (q * wst_ref[...]).astype(BF16)
    ke = (k * wend_ref[...]).astype(BF16)
    qb = q.astype(BF16)
    kb = k.astype(BF16)
    onorm = on_ref[...]
    for h in range(N_HEADS):
        hs = _head_slice(h)
        v = p_ref[0, :, 2 * gw + h * HEAD_DIM:2 * gw + (h + 1) * HEAD_DIM]
        scores = (_dot_nt(qb[:, hs], kb[:, hs]) * dmat_ref[h]).astype(BF16)
        s_h = state[h]
        o = _dot(scores, v) + _dot(qs[:, hs], s_h.astype(BF16))
        state[h] = gdec_ref[h] * s_h + _dot_tn(ke[:, hs], v)
        oc = o - jnp.mean(o, axis=-1, keepdims=True)
        o = oc * lax.rsqrt(jnp.mean(oc * oc, axis=-1, keepdims=True) + EPS) * onorm
        gate = p_ref[0, :, 3 * gw + h * HEAD_DIM:3 * gw + (h + 1) * HEAD_DIM].astype(F32)
        o_ref[0, :, hs] = (o * _silu(gate)).astype(o_ref.dtype)


def _ret(p, cos, sin, dmat, wst, wend, gdec, onorm, t):
    b, s, _ = p.shape
    gw = GROUP_WIDTH
    return pl.pallas_call(
        _ret_kernel,
        grid=(b, s // t),
        in_specs=[
            pl.BlockSpec((1, t, 4 * gw), lambda i, j: (i, j, 1)),
            pl.BlockSpec((t, gw), lambda i, j: (j, 0)),
            pl.BlockSpec((t, gw), lambda i, j: (j, 0)),
            pl.BlockSpec((N_HEADS, t, t), lambda i, j: (0, 0, 0)),
            pl.BlockSpec((t, gw), lambda i, j: (0, 0)),
            pl.BlockSpec((t, gw), lambda i, j: (0, 0)),
            pl.BlockSpec(memory_space=pltpu.SMEM),
            pl.BlockSpec((1, HEAD_DIM), lambda i, j: (0, 0)),
        ],
        out_specs=pl.BlockSpec((1, t, gw), lambda i, j: (i, j, 0)),
        out_shape=jax.ShapeDtypeStruct((b, s, gw), BF16),
        scratch_shapes=[pltpu.VMEM((N_HEADS, HEAD_DIM, HEAD_DIM), F32)],
        compiler_params=pltpu.CompilerParams(
            dimension_semantics=("parallel", "arbitrary"), vmem_limit_bytes=VMEM_LIMIT),
        name="retention",
    )(p, cos, sin, dmat, wst, wend, gdec, onorm)


def _fox_prep_kernel(p_ref, ps_ref, qn_ref, kn_ref, fb_ref, utri_ref,
                     q_out, k_out, v_out, c_out, carry):
    gw = GROUP_WIDTH

    @pl.when(pl.program_id(1) == 0)
    def _():
        carry[...] = jnp.zeros_like(carry)

    qn = qn_ref[...]
    kn = kn_ref[...]
    ones_col = (lax.broadcasted_iota(jnp.int32, (p_ref.shape[1], HEAD_DIM), 1) == 0).astype(BF16)
    for h in range(N_HEADS):
        q = p_ref[0, :, h * HEAD_DIM:(h + 1) * HEAD_DIM].astype(F32)
        k = p_ref[0, :, gw + h * HEAD_DIM:gw + (h + 1) * HEAD_DIM].astype(F32)
        q = q * lax.rsqrt(jnp.mean(q * q, axis=-1, keepdims=True) + EPS) * qn * HEAD_DIM ** -0.5
        k = k * lax.rsqrt(jnp.mean(k * k, axis=-1, keepdims=True) + EPS) * kn
        q_out[0, h] = q.astype(BF16)
        k_out[0, h] = k.astype(BF16)
        v_out[0, h, :, 0:HEAD_DIM] = p_ref[0, :, 2 * gw + h * HEAD_DIM:2 * gw + (h + 1) * HEAD_DIM]
        v_out[0, h, :, HEAD_DIM:2 * HEAD_DIM] = ones_col

    x = ps_ref[0] + fb_ref[...]
    logf = jnp.minimum(x, 0.0) - jnp.log1p(jnp.exp(-jnp.abs(x)))
    logf_t = logf.T[8:16, :]
    c = _dot(logf_t, utri_ref[...], HI) + carry[:, 0:1]
    c_out[0] = c
    t = c.shape[1]
    carry[...] = jnp.broadcast_to(c[:, t - 1:t], carry.shape)


def _fox_prep(p, ps, qn, kn, fb_row, utri, t):
    b, s, _ = p.shape
    gw = GROUP_WIDTH
    hm = jax.ShapeDtypeStruct((b, N_HEADS, s, HEAD_DIM), BF16)
    hm_spec = pl.BlockSpec((1, N_HEADS, t, HEAD_DIM), lambda i, j: (i, 0, j, 0))
    hv = jax.ShapeDtypeStruct((b, N_HEADS, s, 2 * HEAD_DIM), BF16)
    hv_spec = pl.BlockSpec((1, N_HEADS, t, 2 * HEAD_DIM), lambda i, j: (i, 0, j, 0))
    return pl.pallas_call(
        _fox_prep_kernel,
        grid=(b, s // t),
        in_specs=[
            pl.BlockSpec((1, t, 4 * gw), lambda i, j: (i, j, 2)),
            pl.BlockSpec((1, t, SMALL_W), lambda i, j: (i, j, 0)),
            pl.BlockSpec((1, HEAD_DIM), lambda i, j: (0, 0)),
            pl.BlockSpec((1, HEAD_DIM), lambda i, j: (0, 0)),
            pl.BlockSpec((1, SMALL_W), lambda i, j: (0, 0)),
            pl.BlockSpec((t, t), lambda i, j: (0, 0)),
        ],
        out_specs=[hm_spec, hm_spec, hv_spec, pl.BlockSpec((1, 8, t), lambda i, j: (i, 0, j))],
        out_shape=[hm, hm, hv, jax.ShapeDtypeStruct((b, 8, s), F32)],
        scratch_shapes=[pltpu.VMEM((8, 128), F32)],
        compiler_params=pltpu.CompilerParams(
            dimension_semantics=("parallel", "arbitrary"), vmem_limit_bytes=VMEM_LIMIT),
        name="fox_prep",
    )(p, ps, qn, kn, fb_row, utri)


FOX_STRIP = 32
LANES = 128


def _fox_kernel(q_ref, k_ref, v_ref, c_ref, g_ref, o_ref, m_s, acc_s, s_scr, p_scr, al_scr):
    qi = pl.program_id(1)
    ki = pl.program_id(2)
    tq = q_ref.shape[2]
    tk = k_ref.shape[2]
    nj = tk // LANES

    @pl.when(ki == 0)
    def _():
        m_s[...] = jnp.full_like(m_s, NEG_BIG)
        acc_s[...] = jnp.zeros_like(acc_s)

    def update(masked):
        for h in range(N_HEADS):
            s_scr[h] = _dot_nt(q_ref[0, h], k_ref[0, h])
        row = lax.broadcasted_iota(jnp.int32, (FOX_STRIP, LANES), 0)
        col = lax.broadcasted_iota(jnp.int32, (FOX_STRIP, LANES), 1)
        for h in range(N_HEADS):
            c_blk = [c_ref[0, h:h + 1, j * LANES:(j + 1) * LANES] for j in range(nj)]
            for i in range(tq // FOX_STRIP):
                r0 = i * FOX_STRIP
                r = slice(r0, r0 + FOX_STRIP)
                live = min(nj, (r0 + FOX_STRIP - 1) // LANES + 1) if masked else nj
                sb = [s_scr[h, r, j * LANES:(j + 1) * LANES] - c_blk[j] for j in range(live)]
                if masked:
                    sb = [jnp.where(row + r0 >= col + j * LANES, sb[j], NEG_BIG)
                          if (j + 1) * LANES - 1 > r0 else sb[j] for j in range(live)]
                mx = sb[0]
                for j in range(1, live):
                    mx = jnp.maximum(mx, sb[j])
                m_old = m_s[h, r, :]
                m_new = jnp.maximum(m_old, jnp.max(mx, axis=-1, keepdims=True))
                m_s[h, r, :] = m_new
                al_scr[h, r, :] = jnp.exp(m_old - m_new)
                for j in range(live):
                    p_scr[h, r, j * LANES:(j + 1) * LANES] = jnp.exp(sb[j] - m_new).astype(BF16)
                for j in range(live, nj):
                    p_scr[h, r, j * LANES:(j + 1) * LANES] = jnp.zeros((FOX_STRIP, LANES), BF16)
        for h in range(N_HEADS):
            acc_s[h] = al_scr[h] * acc_s[h] + _dot(p_scr[h], v_ref[0, h])

    @pl.when(ki < qi)
    def _():
        update(False)

    @pl.when(ki == qi)
    def _():
        update(True)
        for h in range(N_HEADS):
            hs = _head_slice(h)
            acc = acc_s[h]
            o = acc[:, 0:HEAD_DIM] / acc[:, HEAD_DIM:HEAD_DIM + 1]
            gate = g_ref[0, :, hs].astype(F32)
            o_ref[0, :, hs] = (o * _sigmoid(gate)).astype(o_ref.dtype)


def _fox(qh, kh, vh, c, p, t):
    b, _, s, _ = qh.shape
    gw = GROUP_WIDTH
    n = s // t
    kidx = lambda i, j, k: (i, 0, jnp.minimum(k, j), 0)
    return pl.pallas_call(
        _fox_kernel,
        grid=(b, n, n),
        in_specs=[
            pl.BlockSpec((1, N_HEADS, t, HEAD_DIM), lambda i, j, k: (i, 0, j, 0)),
            pl.BlockSpec((1, N_HEADS, t, HEAD_DIM), kidx),
            pl.BlockSpec((1, N_HEADS, t, 2 * HEAD_DIM), kidx),
            pl.BlockSpec((1, 8, t), lambda i, j, k: (i, 0, jnp.minimum(k, j))),
            pl.BlockSpec((1, t, gw), lambda i, j, k: (i, j, 4 * 2 + 3)),
        ],
        out_specs=pl.BlockSpec((1, t, gw), lambda i, j, k: (i, j, 0)),
        out_shape=jax.ShapeDtypeStruct((b, s, gw), BF16),
        scratch_shapes=[
            pltpu.VMEM((N_HEADS, t, LANES), F32),
            pltpu.VMEM((N_HEADS, t, 2 * HEAD_DIM), F32),
            pltpu.VMEM((N_HEADS, t, t), F32),
            pltpu.VMEM((N_HEADS, t, t), BF16),
            pltpu.VMEM((N_HEADS, t, LANES), F32),
        ],
        compiler_params=pltpu.CompilerParams(
            dimension_semantics=("parallel", "parallel", "arbitrary"), vmem_limit_bytes=VMEM_LIMIT),
        name="fox_attn",
    )(qh, kh, vh, c, p)


N_LEVELS = 6


def _hgrn_tables():
    c = CHUNK
    mall = np.zeros((N_LEVELS + 2, c, c), np.float32)
    masks = np.zeros((N_LEVELS + 1, c, c), np.float32)
    for lv in range(N_LEVELS):
        half = 1 << lv
        for t in range(c):
            m = (t >> (lv + 1)) * (2 * half) + half
            if t >= m:
                mall[lv, t, m:t + 1] = 1.0
            else:
                mall[lv, t, t + 1:m] = 1.0
            for s in range(c):
                if (s >> (lv + 1)) == (t >> (lv + 1)) and t >= m and s < m:
                    masks[lv, t, s] = 1.0
    masks[N_LEVELS] = np.eye(c, dtype=np.float32)
    for t in range(c):
        mall[N_LEVELS, t, :t + 1] = 1.0
        mall[N_LEVELS + 1, t, t + 1:] = 1.0
    return mall.reshape((N_LEVELS + 2) * c, c), masks


def _hgrn_kernel(p_ref, lb_ref, on_ref, mall_ref, mask_ref, o_ref, state):
    t = p_ref.shape[1]
    gw = GROUP_WIDTH

    @pl.when(pl.program_id(1) == 0)
    def _():
        state[...] = jnp.zeros_like(state)

    lb = lb_ref[...]
    onorm = on_ref[...]
    mall = mall_ref[...]
    nl = N_LEVELS + 1
    lvl_mask = jnp.concatenate([mask_ref[...]] * N_HEADS, axis=0)

    def chunk_body(c, carry):
        r = pl.ds(pl.multiple_of(c * CHUNK, CHUNK), CHUNK)
        qx = p_ref[0, r, 0:gw].astype(F32)
        f = p_ref[0, r, gw:2 * gw].astype(F32)
        logf = jnp.log(lb + (1.0 - lb) * _sigmoid(f))
        kk = (1.0 - lb) * _sigmoid(-f)
        qq = _silu(qx)
        x_all = jnp.exp(_dot(mall, logf.astype(BF16)))
        x_q = x_all[N_LEVELS * CHUNK:(N_LEVELS + 1) * CHUNK, :]
        x_k = x_all[(N_LEVELS + 1) * CHUNK:(N_LEVELS + 2) * CHUNK, :]
        qb = qq.astype(BF16)
        kb = kk.astype(BF16)
        ql = [(qq * x_all[lv * CHUNK:(lv + 1) * CHUNK, :]).astype(BF16) for lv in range(N_LEVELS)] + [qb]
        kl = [(kk * x_all[lv * CHUNK:(lv + 1) * CHUNK, :]).astype(BF16) for lv in range(N_LEVELS)] + [kb]
        qc = (qq * x_q).astype(BF16)
        kc = (kk * x_k).astype(BF16)
        qs = jnp.stack([ql[lv][:, _head_slice(h)] for h in range(N_HEADS) for lv in range(nl)])
        ks = jnp.stack([kl[lv][:, _head_slice(h)] for h in range(N_HEADS) for lv in range(nl)])
        sc = _bdot_nt(qs, ks) * lvl_mask
        for h in range(N_HEADS):
            hs = _head_slice(h)
            scores = sc[h * nl]
            for lv in range(1, nl):
                scores = scores + sc[h * nl + lv]
            v = p_ref[0, r, 2 * gw + h * HEAD_DIM:2 * gw + (h + 1) * HEAD_DIM]
            st = state[h]
            o = _dot(scores.astype(BF16), v) + _dot_nt(qc[:, hs], st.astype(BF16))
            state[h] = st * x_q[CHUNK - 1:CHUNK, hs] + _dot_tn(v, kc[:, hs])
            o = o * lax.rsqrt(jnp.mean(o * o, axis=-1, keepdims=True) + EPS) * onorm
            gate = p_ref[0, r, 3 * gw + h * HEAD_DIM:3 * gw + (h + 1) * HEAD_DIM].astype(F32)
            o_ref[0, r, hs] = (o * _silu(gate)).astype(o_ref.dtype)
        return carry

    lax.fori_loop(0, t // CHUNK, chunk_body, 0, unroll=2)


def _hgrn(p, lb_row, onorm, mall, masks, t=512):
    b, s, _ = p.shape
    gw = GROUP_WIDTH
    return pl.pallas_call(
        _hgrn_kernel,
        grid=(b, s // t),
        in_specs=[
            pl.BlockSpec((1, t, 4 * gw), lambda i, j: (i, j, 3)),
            pl.BlockSpec((1, gw), lambda i, j: (0, 0)),
            pl.BlockSpec((1, HEAD_DIM), lambda i, j: (0, 0)),
            pl.BlockSpec(mall.shape, lambda i, j: (0, 0)),
            pl.BlockSpec(masks.shape, lambda i, j: (0, 0, 0)),
        ],
        out_specs=pl.BlockSpec((1, t, gw), lambda i, j: (i, j, 0)),
        out_shape=jax.ShapeDtypeStruct((b, s, gw), BF16),
        scratch_shapes=[pltpu.VMEM((N_HEADS, HEAD_DIM, HEAD_DIM), F32)],
        compiler_params=pltpu.CompilerParams(
            dimension_semantics=("parallel", "arbitrary"), vmem_limit_bytes=VMEM_LIMIT),
        name="hgrn2",
    )(p, lb_row, onorm, mall, masks)


def _post_kernel(h_ref, ya_ref, yb_ref, yc_ref, yd_ref, wo_ref, g_ref, wg_ref, wv_ref,
                 cg_ref, cv_ref, wd_ref, o_ref, buf_g, buf_v, un_s, act_s, *, tf):
    tm = h_ref.shape[1]
    gw = GROUP_WIDTH
    d_ff = wg_ref.shape[1]

    @pl.when(pl.program_id(1) == 0)
    def _():
        buf_g[0:HALO, :] = jnp.zeros((HALO, d_ff), F32)
        buf_v[0:HALO, :] = jnp.zeros((HALO, d_ff), F32)

    h1 = h_ref[0]
    for g, y_ref in enumerate((ya_ref, yb_ref, yc_ref, yd_ref)):
        h1 = h1 + _dot(y_ref[0], wo_ref[g * gw:(g + 1) * gw, :])
    ms = jnp.mean(h1 * h1, axis=-1, keepdims=True)
    un_s[...] = (h1 * lax.rsqrt(ms + EPS) * g_ref[...]).astype(BF16)
    o_ref[0] = h1

    def conv(up, buf, cw, fs):
        buf[HALO:HALO + tm, fs] = up
        out = cw[FFN_CONV - 1:FFN_CONV, fs] * up
        for j in range(FFN_CONV - 1):
            off = HALO - (FFN_CONV - 1) + j
            out = out + cw[j:j + 1, fs] * buf[off:off + tm, fs]
        buf[0:HALO, fs] = up[tm - HALO:tm, :]
        return out

    n_f = d_ff // tf
    split = (n_f // 2) * tf
    for f in range(n_f):
        fs = slice(f * tf, (f + 1) * tf)
        cg = conv(_dot(un_s[...], wg_ref[:, fs]), buf_g, cg_ref, fs)
        cv = conv(_dot(un_s[...], wv_ref[:, fs]), buf_v, cv_ref, fs)
        act_s[:, fs] = (_silu(cg) * cv).astype(BF16)
        if (f + 1) * tf == split:
            o_ref[0] += _dot(act_s[:, 0:split], wd_ref[0:split, :])
    o_ref[0] += _dot(act_s[:, split:d_ff], wd_ref[split:d_ff, :])


def _post(h, ya, yb, yc, yd, wo, g, wg, wv, cg, cv, wd, tm=512, tf=256):
    b, s, d = h.shape
    gw = GROUP_WIDTH
    d_ff = wg.shape[1]
    const = lambda shape: pl.BlockSpec(shape, lambda i, j: (0,) * len(shape))
    y_spec = pl.BlockSpec((1, tm, gw), lambda i, j: (i, j, 0))
    return pl.pallas_call(
        partial(_post_kernel, tf=tf),
        grid=(b, s // tm),
        in_specs=[
            pl.BlockSpec((1, tm, d), lambda i, j: (i, j, 0)),
            y_spec, y_spec, y_spec, y_spec,
            const((N_GROUPS * gw, d)),
            const((1, d)),
            const((d, d_ff)),
            const((d, d_ff)),
            const((FFN_CONV, d_ff)),
            const((FFN_CONV, d_ff)),
            const((d_ff, d)),
        ],
        out_specs=pl.BlockSpec((1, tm, d), lambda i, j: (i, j, 0)),
        out_shape=jax.ShapeDtypeStruct((b, s, d), F32),
        scratch_shapes=[
            pltpu.VMEM((tm + HALO, d_ff), F32),
            pltpu.VMEM((tm + HALO, d_ff), F32),
            pltpu.VMEM((tm, d), BF16),
            pltpu.VMEM((tm, d_ff), BF16),
        ],
        compiler_params=pltpu.CompilerParams(
            dimension_semantics=("parallel", "arbitrary"), vmem_limit_bytes=VMEM_LIMIT),
        name="post_ffn",
    )(h, ya, yb, yc, yd, wo, g, wg, wv, cg, cv, wd)


def _retention_tables(seq, t):
    hd = HEAD_DIM
    inv_freq = ROPE_BASE ** (-jnp.arange(0, hd, 2, dtype=F32) / hd)
    ang = jnp.arange(seq, dtype=F32)[:, None] * inv_freq[None, :]
    cos, sin = jnp.cos(ang), jnp.sin(ang)
    cos_t = jnp.tile(jnp.concatenate([cos, cos], axis=-1), (1, N_HEADS))
    sin_t = jnp.tile(jnp.concatenate([-sin, sin], axis=-1), (1, N_HEADS))
    lgh = jnp.log1p(-jnp.exp2(-RET_DECAY_EXP - jnp.arange(N_HEADS, dtype=F32)))
    n = jnp.arange(t, dtype=F32)
    diff = n[:, None] - n[None, :]
    keep = diff >= 0
    dmat = jnp.where(keep[None], jnp.exp(jnp.where(keep, diff, 0.0)[None] * lgh[:, None, None]), 0.0)
    wst = jnp.repeat(jnp.exp((n + 1.0)[:, None] * lgh[None, :]), hd, axis=1)
    wend = jnp.repeat(jnp.exp((t - 1.0 - n)[:, None] * lgh[None, :]), hd, axis=1)
    gdec = jnp.exp(t * lgh)
    return cos_t, sin_t, dmat, wst, wend, gdec


def kernel(x, norm_mix, norm_ffn, w_in, conv_qkv_a, a_log_a, dt_bias_a, onorm_a, onorm_b, qnorm_c, knorm_c,
           fbias_c, lower_bound_d, onorm_d, w_out, w_up, conv_ffn, w_down):
    b, s, d = x.shape
    depth = w_in.shape[0]
    gw = GROUP_WIDTH
    nh = N_HEADS
    d_ff = w_down.shape[1]
    t_ret = 256
    t_fox = 512

    oa = 4 * gw
    ob = oa + 2 * nh
    oc = ob + 4 * gw
    od = oc + 4 * gw + nh
    w_main = jnp.concatenate(
        [w_in[:, :, 0:oa], w_in[:, :, ob:ob + 4 * gw], w_in[:, :, oc:oc + 4 * gw], w_in[:, :, od:od + 4 * gw]],
        axis=-1).astype(BF16)
    w_small = jnp.concatenate([w_in[:, :, oa:ob], w_in[:, :, oc + 4 * gw:od]], axis=-1)
    w_small = jnp.pad(w_small, ((0, 0), (0, 0), (0, SMALL_W - 3 * nh))).astype(BF16)
    w_out_b = w_out.astype(BF16)
    w_g = w_up[:, :, :d_ff].astype(BF16)
    w_v = w_up[:, :, d_ff:].astype(BF16)
    w_down_b = w_down.astype(BF16)

    def small_row(vals, off):
        return jnp.zeros((depth, 1, SMALL_W), F32).at[:, 0, off:off + nh].set(vals.astype(F32))

    alog_rows = small_row(a_log_a, nh)
    dtb_rows = small_row(dt_bias_a, nh)
    fb_rows = small_row(fbias_c, 2 * nh)

    lbs = jax.nn.softmax(lower_bound_d.astype(F32), axis=0)
    lbs = jnp.cumsum(lbs, axis=0) - lbs[0]

    ltri = jnp.asarray(np.tril(np.ones((CHUNK, CHUNK), np.float32))).astype(BF16)
    utri = jnp.asarray(np.triu(np.ones((CHUNK, CHUNK), np.float32))).astype(BF16)
    utri_fox = jnp.asarray(np.triu(np.ones((t_fox, t_fox), np.float32)))
    mall_np, masks_np = _hgrn_tables()
    mall = jnp.asarray(mall_np).astype(BF16)
    masks = jnp.asarray(masks_np)
    cos_t, sin_t, dmat, wst, wend, gdec = _retention_tables(s, t_ret)

    h = x.astype(F32)
    for l in range(depth):
        pm, ps = _inproj(h.reshape(b * s, d), norm_mix[l].reshape(1, d).astype(F32), w_main[l], w_small[l])
        pm = pm.reshape(b, s, 4 * 4 * gw)
        ps = ps.reshape(b, s, SMALL_W)
        ya = _gdn(pm, ps, conv_qkv_a[l].astype(F32), alog_rows[l], dtb_rows[l],
                  jnp.tile(onorm_a[l].reshape(1, HEAD_DIM).astype(F32), (1, nh)), ltri, utri)
        yb = _ret(pm, cos_t, sin_t, dmat, wst, wend, gdec, onorm_b[l].reshape(1, HEAD_DIM).astype(F32), t_ret)
        qh, kh, vh, c = _fox_prep(pm, ps, qnorm_c[l].reshape(1, HEAD_DIM).astype(F32),
                                  knorm_c[l].reshape(1, HEAD_DIM).astype(F32), fb_rows[l], utri_fox, t_fox)
        yc = _fox(qh, kh, vh, c, pm, t_fox)
        yd = _hgrn(pm, lbs[l].reshape(1, gw), onorm_d[l].reshape(1, HEAD_DIM).astype(F32), mall, masks)
        h = _post(h, ya, yb, yc, yd, w_out_b[l], norm_ffn[l].reshape(1, d).astype(F32), w_g[l], w_v[l],
                  conv_ffn[l][:, :d_ff].astype(F32), conv_ffn[l][:, d_ff:].astype(F32), w_down_b[l])
    return h.astype(x.dtype)
```

```python
from functools import partial

import numpy as np
import jax
import jax.numpy as jnp
from jax import lax
from jax.experimental import pallas as pl
from jax.experimental.pallas import tpu as pltpu

F32 = jnp.float32
BF16 = jnp.bfloat16
HI = lax.Precision.HIGHEST

N_GROUPS = 4
HEAD_DIM = 64
N_HEADS = 4
GROUP_WIDTH = N_HEADS * HEAD_DIM
SHORT_CONV = 4
FFN_CONV = 3
ROPE_BASE = 10000.0
RET_DECAY_EXP = 5.0
EPS = 1e-6
NEG_BIG = -1e30
CHUNK = 64
SMALL_W = 128
HALO = 8
VMEM_LIMIT = 56 * 1024 * 1024


def _dot(a, b, precision=None):
    return jnp.dot(a, b, preferred_element_type=F32, precision=precision)


def _dot_nt(a, b, precision=None):
    return lax.dot_general(a, b, (((1,), (1,)), ((), ())), preferred_element_type=F32, precision=precision)


def _dot_tn(a, b, precision=None):
    return lax.dot_general(a, b, (((0,), (0,)), ((), ())), preferred_element_type=F32, precision=precision)


def _sigmoid(x):
    return 1.0 / (1.0 + jnp.exp(-x))


def _silu(x):
    return x * _sigmoid(x)


def _softplus(x):
    return jnp.maximum(x, 0.0) + jnp.log1p(jnp.exp(-jnp.abs(x)))


def _head_slice(h):
    return slice(h * HEAD_DIM, (h + 1) * HEAD_DIM)


def _inproj_kernel(x_ref, g_ref, wm_ref, ws_ref, pm_ref, ps_ref, un_ref):
    @pl.when(pl.program_id(1) == 0)
    def _():
        x = x_ref[...]
        ms = jnp.mean(x * x, axis=-1, keepdims=True)
        un = (x * lax.rsqrt(ms + EPS) * g_ref[...]).astype(BF16)
        un_ref[...] = un
        ps_ref[...] = _dot(un, ws_ref[...])

    pm_ref[...] = _dot(un_ref[...], wm_ref[...]).astype(BF16)


def _inproj(h2d, g, wm, ws, tm=1024, tn=1024):
    m, d = h2d.shape
    n = wm.shape[1]
    return pl.pallas_call(
        _inproj_kernel,
        grid=(m // tm, n // tn),
        in_specs=[
            pl.BlockSpec((tm, d), lambda i, j: (i, 0)),
            pl.BlockSpec((1, d), lambda i, j: (0, 0)),
            pl.BlockSpec((d, tn), lambda i, j: (0, j)),
            pl.BlockSpec((d, SMALL_W), lambda i, j: (0, 0)),
        ],
        out_specs=[
            pl.BlockSpec((tm, tn), lambda i, j: (i, j)),
            pl.BlockSpec((tm, SMALL_W), lambda i, j: (i, 0)),
        ],
        out_shape=[
            jax.ShapeDtypeStruct((m, n), BF16),
            jax.ShapeDtypeStruct((m, SMALL_W), F32),
        ],
        scratch_shapes=[pltpu.VMEM((tm, d), BF16)],
        compiler_params=pltpu.CompilerParams(
            dimension_semantics=("parallel", "arbitrary"), vmem_limit_bytes=VMEM_LIMIT),
        name="inproj",
    )(h2d, g, wm, ws)


def _dot01(m, x):
    hi = x.astype(BF16)
    lo = (x - hi.astype(F32)).astype(BF16)
    return _dot(m, hi) + _dot(m, lo)


def _bdot(a, b):
    return lax.dot_general(a, b, (((2,), (1,)), ((0,), (0,))), preferred_element_type=F32)


def _bdot_nt(a, b):
    return lax.dot_general(a, b, (((2,), (2,)), ((0,), (0,))), preferred_element_type=F32)


GDN_GROUP = 4


def _dot01_right(x, m):
    hi = x.astype(BF16)
    lo = (x - hi.astype(F32)).astype(BF16)
    return _dot(hi, m) + _dot(lo, m)


def _gdn_kernel(p_ref, ps_ref, cw_ref, alog_ref, dtb_ref, on_ref, ltri_ref, utri_ref,
                o_ref, xbuf, q_s, k_s, v_s, la_s, be_s, g_s, u_s, w_s, qk_s, qg_s, kd_s, state):
    t = p_ref.shape[1]
    gw = GROUP_WIDTH

    @pl.when(pl.program_id(1) == 0)
    def _():
        xbuf[0:HALO, :] = jnp.zeros((HALO, 3 * gw), F32)
        state[...] = jnp.zeros_like(state)

    x = p_ref[0, :, 0:3 * gw].astype(F32)
    xbuf[HALO:HALO + t, :] = x
    cw = cw_ref[...]
    y = cw[3:4, :] * x
    for j in range(SHORT_CONV - 1):
        off = HALO - (SHORT_CONV - 1) + j
        y = y + cw[j:j + 1, :] * xbuf[off:off + t, :]
    xbuf[0:HALO, :] = x[t - HALO:t, :]
    y = _silu(y)
    brow = lax.broadcasted_iota(jnp.int32, (gw, gw), 0) // HEAD_DIM
    bcol = lax.broadcasted_iota(jnp.int32, (gw, gw), 1) // HEAD_DIM
    same_head = brow == bcol
    head_ones = same_head.astype(BF16)
    q = y[:, 0:gw]
    k = y[:, gw:2 * gw]
    q_s[...] = q * lax.rsqrt(_dot01_right(q * q, head_ones) + EPS) * HEAD_DIM ** -0.5
    k_s[...] = k * lax.rsqrt(_dot01_right(k * k, head_ones) + EPS)
    v_s[...] = y[:, 2 * gw:3 * gw]

    small = ps_ref[0]
    la_s[...] = -jnp.exp(alog_ref[...]) * _softplus(small + dtb_ref[...])
    be_s[...] = _sigmoid(small)

    row = lax.broadcasted_iota(jnp.int32, (CHUNK, CHUNK), 0)
    col = lax.broadcasted_iota(jnp.int32, (CHUNK, CHUNK), 1)
    tril = row >= col
    strict = row > col
    eye = (row == col).astype(F32)
    ones8 = jnp.ones((8, CHUNK), BF16)
    ltri = ltri_ref[...]
    utri = utri_ref[...].astype(F32)

    def factor_body(ci, carry):
        rows, qs, ks, vs, betas, gcs, grs = [], [], [], [], [], [], []
        for g in range(GDN_GROUP):
            r = pl.ds(pl.multiple_of((ci * GDN_GROUP + g) * CHUNK, CHUNK), CHUNK)
            rows.append(r)
            la_c = la_s[r, :]
            gcol_all = _dot01(ltri, la_c)
            g_s[r, :] = gcol_all
            be_c = be_s[r, :]
            for h in range(N_HEADS):
                hs = _head_slice(h)
                qs.append(q_s[r, hs])
                ks.append(k_s[r, hs])
                vs.append(v_s[r, hs])
                betas.append(be_c[:, h:h + 1])
                gcs.append(gcol_all[:, N_HEADS + h:N_HEADS + h + 1])
                la_col = la_c[:, N_HEADS + h:N_HEADS + h + 1]
                grs.append(_dot01(ones8, la_col * utri)[0:1, :])
        q = jnp.stack(qs)
        k = jnp.stack(ks)
        v = jnp.stack(vs)
        beta = jnp.stack(betas)
        gc = jnp.stack(gcs)
        gr = jnp.stack(grs)
        gam = jnp.where(tril, jnp.exp(jnp.where(tril, gc - gr, 0.0)), 0.0)
        kb = k * beta
        kbf = k.astype(BF16)
        a_mat = jnp.where(strict, _bdot_nt(kb.astype(BF16), kbf) * gam, 0.0)
        tinv = eye - a_mat
        pw = a_mat
        for _ in range(5):
            pwb = pw.astype(BF16)
            pw = _bdot(pwb, pwb)
            tinv = tinv + _bdot(tinv.astype(BF16), pw.astype(BF16))
        tb = tinv.astype(BF16)
        eg = jnp.exp(gc)
        gend = gc[:, CHUNK - 1:CHUNK, :]
        u = _bdot(tb, (v * beta).astype(BF16))
        w = _bdot(tb, (kb * eg).astype(BF16)).astype(BF16)
        qk = (_bdot_nt(q.astype(BF16), kbf) * gam).astype(BF16)
        qg = (q * eg).astype(BF16)
        kd = (k * jnp.exp(gend - gc)).astype(BF16)
        for g in range(GDN_GROUP):
            for h in range(N_HEADS):
                i = g * N_HEADS + h
                hs = _head_slice(h)
                u_s[rows[g], hs] = u[i]
                w_s[rows[g], hs] = w[i]
                qk_s[rows[g], hs] = qk[i]
                qg_s[rows[g], hs] = qg[i]
                kd_s[rows[g], hs] = kd[i]
        return carry

    lax.fori_loop(0, t // (CHUNK * GDN_GROUP), factor_body, 0)

    lane_head = lax.broadcasted_iota(jnp.int32, (1, gw), 1) // HEAD_DIM
    onorm = on_ref[...]

    def scan_body(c, carry):
        r = pl.ds(pl.multiple_of(c * CHUNK, CHUNK), CHUNK)
        s_f = state[...]
        s_b = s_f.astype(BF16)
        v_new = u_s[r, :] - _dot(w_s[r, :], s_b)
        vnb = v_new.astype(BF16)
        v_bd = jnp.where(same_head, jnp.concatenate([vnb] * N_HEADS, axis=0), jnp.zeros((), BF16))
        o = _dot(qg_s[r, :], s_b) + _dot(qk_s[r, :], v_bd)
        gend = g_s[r, :][CHUNK - 1:CHUNK, :]
        ge_row = jnp.zeros((1, gw), F32)
        for h in range(N_HEADS):
            ge_row = jnp.where(lane_head == h, jnp.exp(gend[:, N_HEADS + h:N_HEADS + h + 1]), ge_row)
        state[...] = s_f * ge_row + jnp.where(same_head, _dot_tn(kd_s[r, :], vnb), 0.0)
        ms = _dot01_right(o * o, head_ones) * (1.0 / HEAD_DIM)
        gate = p_ref[0, r, 3 * gw:4 * gw].astype(F32)
        o_ref[0, r, :] = (o * lax.rsqrt(ms + EPS) * onorm * _silu(gate)).astype(o_ref.dtype)
        return carry

    lax.fori_loop(0, t // CHUNK, scan_body, 0, unroll=2)


def _gdn(p, ps, cw, alog_row, dtb_row, onorm_t, ltri, utri, t=512):
    b, s, _ = p.shape
    gw = GROUP_WIDTH
    const = lambda shape: pl.BlockSpec(shape, lambda i, j: (0,) * len(shape))
    return pl.pallas_call(
        _gdn_kernel,
        grid=(b, s // t),
        in_specs=[
            pl.BlockSpec((1, t, 4 * gw), lambda i, j: (i, j, 0)),
            pl.BlockSpec((1, t, SMALL_W), lambda i, j: (i, j, 0)),
            const((SHORT_CONV, 3 * gw)),
            const((1, SMALL_W)),
            const((1, SMALL_W)),
            const((1, gw)),
            const((CHUNK, CHUNK)),
            const((CHUNK, CHUNK)),
        ],
        out_specs=pl.BlockSpec((1, t, gw), lambda i, j: (i, j, 0)),
        out_shape=jax.ShapeDtypeStruct((b, s, gw), BF16),
        scratch_shapes=[
            pltpu.VMEM((t + HALO, 3 * gw), F32),
            pltpu.VMEM((t, gw), F32),
            pltpu.VMEM((t, gw), F32),
            pltpu.VMEM((t, gw), F32),
            pltpu.VMEM((t, SMALL_W), F32),
            pltpu.VMEM((t, SMALL_W), F32),
            pltpu.VMEM((t, SMALL_W), F32),
            pltpu.VMEM((t, gw), F32),
            pltpu.VMEM((t, gw), BF16),
            pltpu.VMEM((t, gw), BF16),
            pltpu.VMEM((t, gw), BF16),
            pltpu.VMEM((t, gw), BF16),
            pltpu.VMEM((gw, gw), F32),
        ],
        compiler_params=pltpu.CompilerParams(
            dimension_semantics=("parallel", "arbitrary"), vmem_limit_bytes=VMEM_LIMIT),
        name="gdn",
    )(p, ps, cw, alog_row, dtb_row, onorm_t, ltri, utri)


def _ret_kernel(p_ref, cos_ref, sin_ref, dmat_ref, wst_ref, wend_ref, gdec_ref, on_ref,
                o_ref, state):
    gw = GROUP_WIDTH

    @pl.when(pl.program_id(1) == 0)
    def _():
        state[...] = jnp.zeros_like(state)

    q = p_ref[0, :, 0:gw].astype(F32)
    k = p_ref[0, :, gw:2 * gw].astype(F32)
    lane = lax.broadcasted_iota(jnp.int32, q.shape, 1)
    first_half = (lane % HEAD_DIM) < (HEAD_DIM // 2)
    cos = cos_ref[...]
    sin = sin_ref[...]

    def rope(x):
        rot = jnp.where(first_half, pltpu.roll(x, gw - HEAD_DIM // 2, 1), pltpu.roll(x, HEAD_DIM // 2, 1))
        return x * cos + rot * sin

    q = rope(q)
    k = rope(k) * HEAD_DIM ** -0.5
    qs = (q * wst_ref[...]).astype(BF16)
    ke = (k * wend_ref[...]).astype(BF16)
    qb = q.astype(BF16)
    kb = k.astype(BF16)
    onorm = on_ref[...]
    for h in range(N_HEADS):
        hs = _head_slice(h)
        v = p_ref[0, :, 2 * gw + h * HEAD_DIM:2 * gw + (h + 1) * HEAD_DIM]
        scores = (_dot_nt(qb[:, hs], kb[:, hs]) * dmat_ref[h]).astype(BF16)
        s_h = state[h]
        o = _dot(scores, v) + _dot(qs[:, hs], s_h.astype(BF16))
        state[h] = gdec_ref[h] * s_h + _dot_tn(ke[:, hs], v)
        oc = o - jnp.mean(o, axis=-1, keepdims=True)
        o = oc * lax.rsqrt(jnp.mean(oc * oc, axis=-1, keepdims=True) + EPS) * onorm
        gate = p_ref[0, :, 3 * gw + h * HEAD_DIM:3 * gw + (h + 1) * HEAD_DIM].astype(F32)
        o_ref[0, :, hs] = (o * _silu(gate)).astype(o_ref.dtype)


def _ret(p, cos, sin, dmat, wst, wend, gdec, onorm, t):
    b, s, _ = p.shape
    gw = GROUP_WIDTH
    return pl.pallas_call(
        _ret_kernel,
        grid=(b, s // t),
        in_specs=[
            pl.BlockSpec((1, t, 4 * gw), lambda i, j: (i, j, 1)),
            pl.BlockSpec((t, gw), lambda i, j: (j, 0)),
            pl.BlockSpec((t, gw), lambda i, j: (j, 0)),
            pl.BlockSpec((N_HEADS, t, t), lambda i, j: (0, 0, 0)),
            pl.BlockSpec((t, gw), lambda i, j: (0, 0)),
            pl.BlockSpec((t, gw), lambda i, j: (0, 0)),
            pl.BlockSpec(memory_space=pltpu.SMEM),
            pl.BlockSpec((1, HEAD_DIM), lambda i, j: (0, 0)),
        ],
        out_specs=pl.BlockSpec((1, t, gw), lambda i, j: (i, j, 0)),
        out_shape=jax.ShapeDtypeStruct((b, s, gw), BF16),
        scratch_shapes=[pltpu.VMEM((N_HEADS, HEAD_DIM, HEAD_DIM), F32)],
        compiler_params=pltpu.CompilerParams(
            dimension_semantics=("parallel", "arbitrary"), vmem_limit_bytes=VMEM_LIMIT),
        name="retention",
    )(p, cos, sin, dmat, wst, wend, gdec, onorm)


def _fox_prep_kernel(p_ref, ps_ref, qn_ref, kn_ref, fb_ref, utri_ref,
                     q_out, k_out, v_out, c_out, carry):
    gw = GROUP_WIDTH

    @pl.when(pl.program_id(1) == 0)
    def _():
        carry[...] = jnp.zeros_like(carry)

    qn = qn_ref[...]
    kn = kn_ref[...]
    ones_col = (lax.broadcasted_iota(jnp.int32, (p_ref.shape[1], HEAD_DIM), 1) == 0).astype(BF16)
    for h in range(N_HEADS):
        q = p_ref[0, :, h * HEAD_DIM:(h + 1) * HEAD_DIM].astype(F32)
        k = p_ref[0, :, gw + h * HEAD_DIM:gw + (h + 1) * HEAD_DIM].astype(F32)
        q = q * lax.rsqrt(jnp.mean(q * q, axis=-1, keepdims=True) + EPS) * qn * HEAD_DIM ** -0.5
        k = k * lax.rsqrt(jnp.mean(k * k, axis=-1, keepdims=True) + EPS) * kn
        q_out[0, h] = q.astype(BF16)
        k_out[0, h] = k.astype(BF16)
        v_out[0, h, :, 0:HEAD_DIM] = p_ref[0, :, 2 * gw + h * HEAD_DIM:2 * gw + (h + 1) * HEAD_DIM]
        v_out[0, h, :, HEAD_DIM:2 * HEAD_DIM] = ones_col

    x = ps_ref[0] + fb_ref[...]
    logf = jnp.minimum(x, 0.0) - jnp.log1p(jnp.exp(-jnp.abs(x)))
    logf_t = logf.T[8:16, :]
    c = _dot(logf_t, utri_ref[...], HI) + carry[:, 0:1]
    c_out[0, 0] = c
    t = c.shape[1]
    carry[...] = jnp.broadcast_to(c[:, t - 1:t], carry.shape)


def _fox_prep(p, ps, qn, kn, fb_row, utri, t):
    b, s, _ = p.shape
    gw = GROUP_WIDTH
    hm = jax.ShapeDtypeStruct((b, N_HEADS, s, HEAD_DIM), BF16)
    hm_spec = pl.BlockSpec((1, N_HEADS, t, HEAD_DIM), lambda i, j: (i, 0, j, 0))
    hv = jax.ShapeDtypeStruct((b, N_HEADS, s, 2 * HEAD_DIM), BF16)
    hv_spec = pl.BlockSpec((1, N_HEADS, t, 2 * HEAD_DIM), lambda i, j: (i, 0, j, 0))
    return pl.pallas_call(
        _fox_prep_kernel,
        grid=(b, s // t),
        in_specs=[
            pl.BlockSpec((1, t, 4 * gw), lambda i, j: (i, j, 2)),
            pl.BlockSpec((1, t, SMALL_W), lambda i, j: (i, j, 0)),
            pl.BlockSpec((1, HEAD_DIM), lambda i, j: (0, 0)),
            pl.BlockSpec((1, HEAD_DIM), lambda i, j: (0, 0)),
            pl.BlockSpec((1, SMALL_W), lambda i, j: (0, 0)),
            pl.BlockSpec((t, t), lambda i, j: (0, 0)),
        ],
        out_specs=[hm_spec, hm_spec, hv_spec, pl.BlockSpec((1, 1, 8, t), lambda i, j: (i, j, 0, 0))],
        out_shape=[hm, hm, hv, jax.ShapeDtypeStruct((b, s // t, 8, t), F32)],
        scratch_shapes=[pltpu.VMEM((8, 128), F32)],
        compiler_params=pltpu.CompilerParams(
            dimension_semantics=("parallel", "arbitrary"), vmem_limit_bytes=VMEM_LIMIT),
        name="fox_prep",
    )(p, ps, qn, kn, fb_row, utri)


FOX_STRIP = 32
LANES = 128


def _fox_kernel(q_ref, k_ref, v_ref, c_ref, g_ref, o_ref, m_s, acc_s, s_scr, p_scr, al_scr):
    qi = pl.program_id(1)
    tq = q_ref.shape[2]
    tk = c_ref.shape[3]
    nj = tk // LANES

    m_s[...] = jnp.full_like(m_s, NEG_BIG)
    acc_s[...] = jnp.zeros_like(acc_s)

    def update(ki, masked):
        kr = pl.ds(pl.multiple_of(ki * tk, tk), tk)
        for h in range(N_HEADS):
            s_scr[h] = _dot_nt(q_ref[0, h], k_ref[0, h, kr, :])
        row = lax.broadcasted_iota(jnp.int32, (FOX_STRIP, LANES), 0)
        col = lax.broadcasted_iota(jnp.int32, (FOX_STRIP, LANES), 1)
        for h in range(N_HEADS):
            c_blk = [c_ref[0, ki, h:h + 1, j * LANES:(j + 1) * LANES] for j in range(nj)]
            for i in range(tq // FOX_STRIP):
                r0 = i * FOX_STRIP
                r = slice(r0, r0 + FOX_STRIP)
                live = min(nj, (r0 + FOX_STRIP - 1) // LANES + 1) if masked else nj
                sb = [s_scr[h, r, j * LANES:(j + 1) * LANES] - c_blk[j] for j in range(live)]
                if masked:
                    sb = [jnp.where(row + r0 >= col + j * LANES, sb[j], NEG_BIG)
                          if (j + 1) * LANES - 1 > r0 else sb[j] for j in range(live)]
                mx = sb[0]
                for j in range(1, live):
                    mx = jnp.maximum(mx, sb[j])
                m_old = m_s[h, r, :]
                m_new = jnp.maximum(m_old, jnp.max(mx, axis=-1, keepdims=True))
                m_s[h, r, :] = m_new
                al_scr[h, r, :] = jnp.exp(m_old - m_new)
                for j in range(live):
                    p_scr[h, r, j * LANES:(j + 1) * LANES] = jnp.exp(sb[j] - m_new).astype(BF16)
                for j in range(live, nj):
                    p_scr[h, r, j * LANES:(j + 1) * LANES] = jnp.zeros((FOX_STRIP, LANES), BF16)
        for h in range(N_HEADS):
            acc_s[h] = al_scr[h] * acc_s[h] + _dot(p_scr[h], v_ref[0, h, kr, :])

    def below_diagonal(ki, carry):
        update(ki, False)
        return carry

    lax.fori_loop(0, qi, below_diagonal, 0)
    update(qi, True)
    for h in range(N_HEADS):
        hs = _head_slice(h)
        acc = acc_s[h]
        o = acc[:, 0:HEAD_DIM] / acc[:, HEAD_DIM:HEAD_DIM + 1]
        gate = g_ref[0, :, hs].astype(F32)
        o_ref[0, :, hs] = (o * _sigmoid(gate)).astype(o_ref.dtype)


def _fox(qh, kh, vh, c, p, t):
    b, _, s, _ = qh.shape
    gw = GROUP_WIDTH
    n = s // t
    return pl.pallas_call(
        _fox_kernel,
        grid=(b, n),
        in_specs=[
            pl.BlockSpec((1, N_HEADS, t, HEAD_DIM), lambda i, j: (i, 0, j, 0)),
            pl.BlockSpec((1, N_HEADS, s, HEAD_DIM), lambda i, j: (i, 0, 0, 0)),
            pl.BlockSpec((1, N_HEADS, s, 2 * HEAD_DIM), lambda i, j: (i, 0, 0, 0)),
            pl.BlockSpec((1, n, 8, t), lambda i, j: (i, 0, 0, 0)),
            pl.BlockSpec((1, t, gw), lambda i, j: (i, j, 4 * 2 + 3)),
        ],
        out_specs=pl.BlockSpec((1, t, gw), lambda i, j: (i, j, 0)),
        out_shape=jax.ShapeDtypeStruct((b, s, gw), BF16),
        scratch_shapes=[
            pltpu.VMEM((N_HEADS, t, LANES), F32),
            pltpu.VMEM((N_HEADS, t, 2 * HEAD_DIM), F32),
            pltpu.VMEM((N_HEADS, t, t), F32),
            pltpu.VMEM((N_HEADS, t, t), BF16),
            pltpu.VMEM((N_HEADS, t, LANES), F32),
        ],
        compiler_params=pltpu.CompilerParams(
            dimension_semantics=("parallel", "parallel"), vmem_limit_bytes=VMEM_LIMIT),
        name="fox_attn",
    )(qh, kh, vh, c, p)


N_LEVELS = 6


def _hgrn_tables():
    c = CHUNK
    mall = np.zeros((N_LEVELS + 2, c, c), np.float32)
    masks = np.zeros((N_LEVELS + 1, c, c), np.float32)
    for lv in range(N_LEVELS):
        half = 1 << lv
        for t in range(c):
            m = (t >> (lv + 1)) * (2 * half) + half
            if t >= m:
                mall[lv, t, m:t + 1] = 1.0
            else:
                mall[lv, t, t + 1:m] = 1.0
            for s in range(c):
                if (s >> (lv + 1)) == (t >> (lv + 1)) and t >= m and s < m:
                    masks[lv, t, s] = 1.0
    masks[N_LEVELS] = np.eye(c, dtype=np.float32)
    for t in range(c):
        mall[N_LEVELS, t, :t + 1] = 1.0
        mall[N_LEVELS + 1, t, t + 1:] = 1.0
    return mall.reshape((N_LEVELS + 2) * c, c), masks


def _hgrn_kernel(p_ref, lb_ref, on_ref, mall_ref, mask_ref, o_ref, state):
    t = p_ref.shape[1]
    gw = GROUP_WIDTH

    @pl.when(pl.program_id(1) == 0)
    def _():
        state[...] = jnp.zeros_like(state)

    lb = lb_ref[...]
    onorm = on_ref[...]
    mall = mall_ref[...]
    nl = N_LEVELS + 1
    lvl_mask = jnp.concatenate([mask_ref[...]] * N_HEADS, axis=0)

    def chunk_body(c, carry):
        r = pl.ds(pl.multiple_of(c * CHUNK, CHUNK), CHUNK)
        qx = p_ref[0, r, 0:gw].astype(F32)
        f = p_ref[0, r, gw:2 * gw].astype(F32)
        logf = jnp.log(lb + (1.0 - lb) * _sigmoid(f))
        kk = (1.0 - lb) * _sigmoid(-f)
        qq = _silu(qx)
        x_all = jnp.exp(_dot(mall, logf.astype(BF16)))
        x_q = x_all[N_LEVELS * CHUNK:(N_LEVELS + 1) * CHUNK, :]
        x_k = x_all[(N_LEVELS + 1) * CHUNK:(N_LEVELS + 2) * CHUNK, :]
        qb = qq.astype(BF16)
        kb = kk.astype(BF16)
        ql = [(qq * x_all[lv * CHUNK:(lv + 1) * CHUNK, :]).astype(BF16) for lv in range(N_LEVELS)] + [qb]
        kl = [(kk * x_all[lv * CHUNK:(lv + 1) * CHUNK, :]).astype(BF16) for lv in range(N_LEVELS)] + [kb]
        qc = (qq * x_q).astype(BF16)
        kc = (kk * x_k).astype(BF16)
        qs = jnp.stack([ql[lv][:, _head_slice(h)] for h in range(N_HEADS) for lv in range(nl)])
        ks = jnp.stack([kl[lv][:, _head_slice(h)] for h in range(N_HEADS) for lv in range(nl)])
        sc = _bdot_nt(qs, ks) * lvl_mask
        for h in range(N_HEADS):
            hs = _head_slice(h)
            scores = sc[h * nl]
            for lv in range(1, nl):
                scores = scores + sc[h * nl + lv]
            v = p_ref[0, r, 2 * gw + h * HEAD_DIM:2 * gw + (h + 1) * HEAD_DIM]
            st = state[h]
            o = _dot(scores.astype(BF16), v) + _dot_nt(qc[:, hs], st.astype(BF16))
            state[h] = st * x_q[CHUNK - 1:CHUNK, hs] + _dot_tn(v, kc[:, hs])
            o = o * lax.rsqrt(jnp.mean(o * o, axis=-1, keepdims=True) + EPS) * onorm
            gate = p_ref[0, r, 3 * gw + h * HEAD_DIM:3 * gw + (h + 1) * HEAD_DIM].astype(F32)
            o_ref[0, r, hs] = (o * _silu(gate)).astype(o_ref.dtype)
        return carry

    lax.fori_loop(0, t // CHUNK, chunk_body, 0, unroll=2)


def _hgrn(p, lb_row, onorm, mall, masks, t=512):
    b, s, _ = p.shape
    gw = GROUP_WIDTH
    return pl.pallas_call(
        _hgrn_kernel,
        grid=(b, s // t),
        in_specs=[
            pl.BlockSpec((1, t, 4 * gw), lambda i, j: (i, j, 3)),
            pl.BlockSpec((1, gw), lambda i, j: (0, 0)),
            pl.BlockSpec((1, HEAD_DIM), lambda i, j: (0, 0)),
            pl.BlockSpec(mall.shape, lambda i, j: (0, 0)),
            pl.BlockSpec(masks.shape, lambda i, j: (0, 0, 0)),
        ],
        out_specs=pl.BlockSpec((1, t, gw), lambda i, j: (i, j, 0)),
        out_shape=jax.ShapeDtypeStruct((b, s, gw), BF16),
        scratch_shapes=[pltpu.VMEM((N_HEADS, HEAD_DIM, HEAD_DIM), F32)],
        compiler_params=pltpu.CompilerParams(
            dimension_semantics=("parallel", "arbitrary"), vmem_limit_bytes=VMEM_LIMIT),
        name="hgrn2",
    )(p, lb_row, onorm, mall, masks)


def _post_kernel(h_ref, ya_ref, yb_ref, yc_ref, yd_ref, wo_ref, g_ref, wg_ref, wv_ref,
                 cg_ref, cv_ref, wd_ref, o_ref, buf_g, buf_v, un_s, act_s, *, tf):
    tm = h_ref.shape[1]
    gw = GROUP_WIDTH
    d_ff = wg_ref.shape[1]

    @pl.when(pl.program_id(1) == 0)
    def _():
        buf_g[0:HALO, :] = jnp.zeros((HALO, d_ff), F32)
        buf_v[0:HALO, :] = jnp.zeros((HALO, d_ff), F32)

    h1 = h_ref[0]
    for g, y_ref in enumerate((ya_ref, yb_ref, yc_ref, yd_ref)):
        h1 = h1 + _dot(y_ref[0], wo_ref[g * gw:(g + 1) * gw, :])
    ms = jnp.mean(h1 * h1, axis=-1, keepdims=True)
    un_s[...] = (h1 * lax.rsqrt(ms + EPS) * g_ref[...]).astype(BF16)
    o_ref[0] = h1

    def conv(up, buf, cw, fs):
        buf[HALO:HALO + tm, fs] = up
        out = cw[FFN_CONV - 1:FFN_CONV, fs] * up
        for j in range(FFN_CONV - 1):
            off = HALO - (FFN_CONV - 1) + j
            out = out + cw[j:j + 1, fs] * buf[off:off + tm, fs]
        buf[0:HALO, fs] = up[tm - HALO:tm, :]
        return out

    n_f = d_ff // tf
    split = (n_f // 2) * tf
    for f in range(n_f):
        fs = slice(f * tf, (f + 1) * tf)
        cg = conv(_dot(un_s[...], wg_ref[:, fs]), buf_g, cg_ref, fs)
        cv = conv(_dot(un_s[...], wv_ref[:, fs]), buf_v, cv_ref, fs)
        act_s[:, fs] = (_silu(cg) * cv).astype(BF16)
        if (f + 1) * tf == split:
            o_ref[0] += _dot(act_s[:, 0:split], wd_ref[0:split, :])
    o_ref[0] += _dot(act_s[:, split:d_ff], wd_ref[split:d_ff, :])


def _post(h, ya, yb, yc, yd, wo, g, wg, wv, cg, cv, wd, tm=512, tf=256):
    b, s, d = h.shape
    gw = GROUP_WIDTH
    d_ff = wg.shape[1]
    const = lambda shape: pl.BlockSpec(shape, lambda i, j: (0,) * len(shape))
    y_spec = pl.BlockSpec((1, tm, gw), lambda i, j: (i, j, 0))
    return pl.pallas_call(
        partial(_post_kernel, tf=tf),
        grid=(b, s // tm),
        in_specs=[
            pl.BlockSpec((1, tm, d), lambda i, j: (i, j, 0)),
            y_spec, y_spec, y_spec, y_spec,
            const((N_GROUPS * gw, d)),
            const((1, d)),
            const((d, d_ff)),
            const((d, d_ff)),
            const((FFN_CONV, d_ff)),
            const((FFN_CONV, d_ff)),
            const((d_ff, d)),
        ],
        out_specs=pl.BlockSpec((1, tm, d), lambda i, j: (i, j, 0)),
        out_shape=jax.ShapeDtypeStruct((b, s, d), F32),
        scratch_shapes=[
            pltpu.VMEM((tm + HALO, d_ff), F32),
            pltpu.VMEM((tm + HALO, d_ff), F32),
            pltpu.VMEM((tm, d), BF16),
            pltpu.VMEM((tm, d_ff), BF16),
        ],
        compiler_params=pltpu.CompilerParams(
            dimension_semantics=("parallel", "arbitrary"), vmem_limit_bytes=VMEM_LIMIT),
        name="post_ffn",
    )(h, ya, yb, yc, yd, wo, g, wg, wv, cg, cv, wd)


def _retention_tables(seq, t):
    hd = HEAD_DIM
    inv_freq = ROPE_BASE ** (-jnp.arange(0, hd, 2, dtype=F32) / hd)
    ang = jnp.arange(seq, dtype=F32)[:, None] * inv_freq[None, :]
    cos, sin = jnp.cos(ang), jnp.sin(ang)
    cos_t = jnp.tile(jnp.concatenate([cos, cos], axis=-1), (1, N_HEADS))
    sin_t = jnp.tile(jnp.concatenate([-sin, sin], axis=-1), (1, N_HEADS))
    lgh = jnp.log1p(-jnp.exp2(-RET_DECAY_EXP - jnp.arange(N_HEADS, dtype=F32)))
    n = jnp.arange(t, dtype=F32)
    diff = n[:, None] - n[None, :]
    keep = diff >= 0
    dmat = jnp.where(keep[None], jnp.exp(jnp.where(keep, diff, 0.0)[None] * lgh[:, None, None]), 0.0)
    wst = jnp.repeat(jnp.exp((n + 1.0)[:, None] * lgh[None, :]), hd, axis=1)
    wend = jnp.repeat(jnp.exp((t - 1.0 - n)[:, None] * lgh[None, :]), hd, axis=1)
    gdec = jnp.exp(t * lgh)
    return cos_t, sin_t, dmat, wst, wend, gdec


def kernel(x, norm_mix, norm_ffn, w_in, conv_qkv_a, a_log_a, dt_bias_a, onorm_a, onorm_b, qnorm_c, knorm_c,
           fbias_c, lower_bound_d, onorm_d, w_out, w_up, conv_ffn, w_down):
    b, s, d = x.shape
    depth = w_in.shape[0]
    gw = GROUP_WIDTH
    nh = N_HEADS
    d_ff = w_down.shape[1]
    t_ret = 256
    t_fox = 512

    oa = 4 * gw
    ob = oa + 2 * nh
    oc = ob + 4 * gw
    od = oc + 4 * gw + nh
    w_main = jnp.concatenate(
        [w_in[:, :, 0:oa], w_in[:, :, ob:ob + 4 * gw], w_in[:, :, oc:oc + 4 * gw], w_in[:, :, od:od + 4 * gw]],
        axis=-1).astype(BF16)
    w_small = jnp.concatenate([w_in[:, :, oa:ob], w_in[:, :, oc + 4 * gw:od]], axis=-1)
    w_small = jnp.pad(w_small, ((0, 0), (0, 0), (0, SMALL_W - 3 * nh))).astype(BF16)
    w_out_b = w_out.astype(BF16)
    w_g = w_up[:, :, :d_ff].astype(BF16)
    w_v = w_up[:, :, d_ff:].astype(BF16)
    w_down_b = w_down.astype(BF16)

    def small_row(vals, off):
        return jnp.zeros((depth, 1, SMALL_W), F32).at[:, 0, off:off + nh].set(vals.astype(F32))

    alog_rows = small_row(a_log_a, nh)
    dtb_rows = small_row(dt_bias_a, nh)
    fb_rows = small_row(fbias_c, 2 * nh)

    lbs = jax.nn.softmax(lower_bound_d.astype(F32), axis=0)
    lbs = jnp.cumsum(lbs, axis=0) - lbs[0]

    ltri = jnp.asarray(np.tril(np.ones((CHUNK, CHUNK), np.float32))).astype(BF16)
    utri = jnp.asarray(np.triu(np.ones((CHUNK, CHUNK), np.float32))).astype(BF16)
    utri_fox = jnp.asarray(np.triu(np.ones((t_fox, t_fox), np.float32)))
    mall_np, masks_np = _hgrn_tables()
    mall = jnp.asarray(mall_np).astype(BF16)
    masks = jnp.asarray(masks_np)
    cos_t, sin_t, dmat, wst, wend, gdec = _retention_tables(s, t_ret)

    h = x.astype(F32)
    for l in range(depth):
        pm, ps = _inproj(h.reshape(b * s, d), norm_mix[l].reshape(1, d).astype(F32), w_main[l], w_small[l])
        pm = pm.reshape(b, s, 4 * 4 * gw)
        ps = ps.reshape(b, s, SMALL_W)
        ya = _gdn(pm, ps, conv_qkv_a[l].astype(F32), alog_rows[l], dtb_rows[l],
                  jnp.tile(onorm_a[l].reshape(1, HEAD_DIM).astype(F32), (1, nh)), ltri, utri)
        yb = _ret(pm, cos_t, sin_t, dmat, wst, wend, gdec, onorm_b[l].reshape(1, HEAD_DIM).astype(F32), t_ret)
        qh, kh, vh, c = _fox_prep(pm, ps, qnorm_c[l].reshape(1, HEAD_DIM).astype(F32),
                                  knorm_c[l].reshape(1, HEAD_DIM).astype(F32), fb_rows[l], utri_fox, t_fox)
        yc = _fox(qh, kh, vh, c, pm, t_fox)
        yd = _hgrn(pm, lbs[l].reshape(1, gw), onorm_d[l].reshape(1, HEAD_DIM).astype(F32), mall, masks)
        h = _post(h, ya, yb, yc, yd, w_out_b[l], norm_ffn[l].reshape(1, d).astype(F32), w_g[l], w_v[l],
                  conv_ffn[l][:, :d_ff].astype(F32), conv_ffn[l][:, d_ff:].astype(F32), w_down_b[l])
    return h.astype(x.dtype)
```

```python
from functools import partial

import numpy as np
import jax
import jax.numpy as jnp
from jax import lax
from jax.experimental import pallas as pl
from jax.experimental.pallas import tpu as pltpu

F32 = jnp.float32
BF16 = jnp.bfloat16
HI = lax.Precision.HIGHEST

N_GROUPS = 4
HEAD_DIM = 64
N_HEADS = 4
GROUP_WIDTH = N_HEADS * HEAD_DIM
SHORT_CONV = 4
FFN_CONV = 3
ROPE_BASE = 10000.0
RET_DECAY_EXP = 5.0
EPS = 1e-6
NEG_BIG = -1e30
CHUNK = 64
SMALL_W = 128
HALO = 8
VMEM_LIMIT = 56 * 1024 * 1024


def _dot(a, b, precision=None):
    return jnp.dot(a, b, preferred_element_type=F32, precision=precision)


def _dot_nt(a, b, precision=None):
    return lax.dot_general(a, b, (((1,), (1,)), ((), ())), preferred_element_type=F32, precision=precision)


def _dot_tn(a, b, precision=None):
    return lax.dot_general(a, b, (((0,), (0,)), ((), ())), preferred_element_type=F32, precision=precision)


def _sigmoid(x):
    return 1.0 / (1.0 + jnp.exp(-x))


def _silu(x):
    return x * _sigmoid(x)


def _softplus(x):
    return jnp.maximum(x, 0.0) + jnp.log1p(jnp.exp(-jnp.abs(x)))


def _head_slice(h):
    return slice(h * HEAD_DIM, (h + 1) * HEAD_DIM)


def _inproj_kernel(x_ref, g_ref, wm_ref, ws_ref, pm_ref, ps_ref, un_ref, *, tn):
    x = x_ref[...]
    ms = jnp.mean(x * x, axis=-1, keepdims=True)
    un_ref[...] = (x * lax.rsqrt(ms + EPS) * g_ref[...]).astype(BF16)
    ps_ref[...] = _dot(un_ref[...], ws_ref[...])
    for j in range(wm_ref.shape[1] // tn):
        ns = slice(j * tn, (j + 1) * tn)
        pm_ref[:, ns] = _dot(un_ref[...], wm_ref[:, ns]).astype(BF16)


def _inproj(h2d, g, wm, ws, tm=1024, tn=1024):
    m, d = h2d.shape
    n = wm.shape[1]
    return pl.pallas_call(
        partial(_inproj_kernel, tn=tn),
        grid=(m // tm,),
        in_specs=[
            pl.BlockSpec((tm, d), lambda i: (i, 0)),
            pl.BlockSpec((1, d), lambda i: (0, 0)),
            pl.BlockSpec((d, n), lambda i: (0, 0)),
            pl.BlockSpec((d, SMALL_W), lambda i: (0, 0)),
        ],
        out_specs=[
            pl.BlockSpec((tm, n), lambda i: (i, 0)),
            pl.BlockSpec((tm, SMALL_W), lambda i: (i, 0)),
        ],
        out_shape=[
            jax.ShapeDtypeStruct((m, n), BF16),
            jax.ShapeDtypeStruct((m, SMALL_W), F32),
        ],
        scratch_shapes=[pltpu.VMEM((tm, d), BF16)],
        compiler_params=pltpu.CompilerParams(
            dimension_semantics=("parallel",), vmem_limit_bytes=VMEM_LIMIT),
        name="inproj",
    )(h2d, g, wm, ws)


def _dot01(m, x):
    hi = x.astype(BF16)
    lo = (x - hi.astype(F32)).astype(BF16)
    return _dot(m, hi) + _dot(m, lo)


def _bdot(a, b):
    return lax.dot_general(a, b, (((2,), (1,)), ((0,), (0,))), preferred_element_type=F32)


def _bdot_nt(a, b):
    return lax.dot_general(a, b, (((2,), (2,)), ((0,), (0,))), preferred_element_type=F32)


GDN_GROUP = 4


def _dot01_right(x, m):
    hi = x.astype(BF16)
    lo = (x - hi.astype(F32)).astype(BF16)
    return _dot(hi, m) + _dot(lo, m)


def _gdn_kernel(p_ref, ps_ref, cw_ref, alog_ref, dtb_ref, on_ref, ltri_ref, utri_ref,
                o_ref, xbuf, q_s, k_s, v_s, la_s, be_s, g_s, u_s, w_s, qk_s, qg_s, kd_s, state):
    t = p_ref.shape[1]
    gw = GROUP_WIDTH

    @pl.when(pl.program_id(1) == 0)
    def _():
        xbuf[0:HALO, :] = jnp.zeros((HALO, 3 * gw), F32)
        state[...] = jnp.zeros_like(state)

    x = p_ref[0, :, 0:3 * gw].astype(F32)
    xbuf[HALO:HALO + t, :] = x
    cw = cw_ref[...]
    y = cw[3:4, :] * x
    for j in range(SHORT_CONV - 1):
        off = HALO - (SHORT_CONV - 1) + j
        y = y + cw[j:j + 1, :] * xbuf[off:off + t, :]
    xbuf[0:HALO, :] = x[t - HALO:t, :]
    y = _silu(y)
    brow = lax.broadcasted_iota(jnp.int32, (gw, gw), 0) // HEAD_DIM
    bcol = lax.broadcasted_iota(jnp.int32, (gw, gw), 1) // HEAD_DIM
    same_head = brow == bcol
    head_ones = same_head.astype(BF16)
    q = y[:, 0:gw]
    k = y[:, gw:2 * gw]
    q_s[...] = q * lax.rsqrt(_dot01_right(q * q, head_ones) + EPS) * HEAD_DIM ** -0.5
    k_s[...] = k * lax.rsqrt(_dot01_right(k * k, head_ones) + EPS)
    v_s[...] = y[:, 2 * gw:3 * gw]

    small = ps_ref[0]
    la_s[...] = -jnp.exp(alog_ref[...]) * _softplus(small + dtb_ref[...])
    be_s[...] = _sigmoid(small)

    row = lax.broadcasted_iota(jnp.int32, (CHUNK, CHUNK), 0)
    col = lax.broadcasted_iota(jnp.int32, (CHUNK, CHUNK), 1)
    tril = row >= col
    strict = row > col
    eye = (row == col).astype(F32)
    ones8 = jnp.ones((8, CHUNK), BF16)
    ltri = ltri_ref[...]
    utri = utri_ref[...].astype(F32)

    def factor_body(ci, carry):
        rows, qs, ks, vs, betas, gcs, grs = [], [], [], [], [], [], []
        for g in range(GDN_GROUP):
            r = pl.ds(pl.multiple_of((ci * GDN_GROUP + g) * CHUNK, CHUNK), CHUNK)
            rows.append(r)
            la_c = la_s[r, :]
            gcol_all = _dot01(ltri, la_c)
            g_s[r, :] = gcol_all
            be_c = be_s[r, :]
            for h in range(N_HEADS):
                hs = _head_slice(h)
                qs.append(q_s[r, hs])
                ks.append(k_s[r, hs])
                vs.append(v_s[r, hs])
                betas.append(be_c[:, h:h + 1])
                gcs.append(gcol_all[:, N_HEADS + h:N_HEADS + h + 1])
                la_col = la_c[:, N_HEADS + h:N_HEADS + h + 1]
                grs.append(_dot01(ones8, la_col * utri)[0:1, :])
        q = jnp.stack(qs)
        k = jnp.stack(ks)
        v = jnp.stack(vs)
        beta = jnp.stack(betas)
        gc = jnp.stack(gcs)
        gr = jnp.stack(grs)
        gam = jnp.where(tril, jnp.exp(jnp.where(tril, gc - gr, 0.0)), 0.0)
        kb = k * beta
        kbf = k.astype(BF16)
        a_mat = jnp.where(strict, _bdot_nt(kb.astype(BF16), kbf) * gam, 0.0)
        tinv = eye - a_mat
        pw = a_mat
        for _ in range(5):
            pwb = pw.astype(BF16)
            pw = _bdot(pwb, pwb)
            tinv = tinv + _bdot(tinv.astype(BF16), pw.astype(BF16))
        tb = tinv.astype(BF16)
        eg = jnp.exp(gc)
        gend = gc[:, CHUNK - 1:CHUNK, :]
        u = _bdot(tb, (v * beta).astype(BF16))
        w = _bdot(tb, (kb * eg).astype(BF16)).astype(BF16)
        qk = (_bdot_nt(q.astype(BF16), kbf) * gam).astype(BF16)
        qg = (q * eg).astype(BF16)
        kd = (k * jnp.exp(gend - gc)).astype(BF16)
        for g in range(GDN_GROUP):
            for h in range(N_HEADS):
                i = g * N_HEADS + h
                hs = _head_slice(h)
                u_s[rows[g], hs] = u[i]
                w_s[rows[g], hs] = w[i]
                qk_s[rows[g], hs] = qk[i]
                qg_s[rows[g], hs] = qg[i]
                kd_s[rows[g], hs] = kd[i]
        return carry

    lax.fori_loop(0, t // (CHUNK * GDN_GROUP), factor_body, 0)

    lane_head = lax.broadcasted_iota(jnp.int32, (1, gw), 1) // HEAD_DIM
    onorm = on_ref[...]

    def scan_body(c, carry):
        r = pl.ds(pl.multiple_of(c * CHUNK, CHUNK), CHUNK)
        s_f = state[...]
        s_b = s_f.astype(BF16)
        v_new = u_s[r, :] - _dot(w_s[r, :], s_b)
        vnb = v_new.astype(BF16)
        v_bd = jnp.where(same_head, jnp.concatenate([vnb] * N_HEADS, axis=0), jnp.zeros((), BF16))
        o = _dot(qg_s[r, :], s_b) + _dot(qk_s[r, :], v_bd)
        gend = g_s[r, :][CHUNK - 1:CHUNK, :]
        ge_row = jnp.zeros((1, gw), F32)
        for h in range(N_HEADS):
            ge_row = jnp.where(lane_head == h, jnp.exp(gend[:, N_HEADS + h:N_HEADS + h + 1]), ge_row)
        state[...] = s_f * ge_row + jnp.where(same_head, _dot_tn(kd_s[r, :], vnb), 0.0)
        ms = _dot01_right(o * o, head_ones) * (1.0 / HEAD_DIM)
        gate = p_ref[0, r, 3 * gw:4 * gw].astype(F32)
        o_ref[0, r, :] = (o * lax.rsqrt(ms + EPS) * onorm * _silu(gate)).astype(o_ref.dtype)
        return carry

    lax.fori_loop(0, t // CHUNK, scan_body, 0, unroll=2)


def _gdn(p, ps, cw, alog_row, dtb_row, onorm_t, ltri, utri, t=512):
    b, s, _ = p.shape
    gw = GROUP_WIDTH
    const = lambda shape: pl.BlockSpec(shape, lambda i, j: (0,) * len(shape))
    return pl.pallas_call(
        _gdn_kernel,
        grid=(b, s // t),
        in_specs=[
            pl.BlockSpec((1, t, 4 * gw), lambda i, j: (i, j, 0)),
            pl.BlockSpec((1, t, SMALL_W), lambda i, j: (i, j, 0)),
            const((SHORT_CONV, 3 * gw)),
            const((1, SMALL_W)),
            const((1, SMALL_W)),
            const((1, gw)),
            const((CHUNK, CHUNK)),
            const((CHUNK, CHUNK)),
        ],
        out_specs=pl.BlockSpec((1, t, gw), lambda i, j: (i, j, 0)),
        out_shape=jax.ShapeDtypeStruct((b, s, gw), BF16),
        scratch_shapes=[
            pltpu.VMEM((t + HALO, 3 * gw), F32),
            pltpu.VMEM((t, gw), F32),
            pltpu.VMEM((t, gw), F32),
            pltpu.VMEM((t, gw), F32),
            pltpu.VMEM((t, SMALL_W), F32),
            pltpu.VMEM((t, SMALL_W), F32),
            pltpu.VMEM((t, SMALL_W), F32),
            pltpu.VMEM((t, gw), F32),
            pltpu.VMEM((t, gw), BF16),
            pltpu.VMEM((t, gw), BF16),
            pltpu.VMEM((t, gw), BF16),
            pltpu.VMEM((t, gw), BF16),
            pltpu.VMEM((gw, gw), F32),
        ],
        compiler_params=pltpu.CompilerParams(
            dimension_semantics=("parallel", "arbitrary"), vmem_limit_bytes=VMEM_LIMIT),
        name="gdn",
    )(p, ps, cw, alog_row, dtb_row, onorm_t, ltri, utri)


def _ret_kernel(p_ref, cos_ref, sin_ref, dmat_ref, wst_ref, wend_ref, gdec_ref, on_ref,
                o_ref, state):
    gw = GROUP_WIDTH

    @pl.when(pl.program_id(1) == 0)
    def _():
        state[...] = jnp.zeros_like(state)

    q = p_ref[0, :, 0:gw].astype(F32)
    k = p_ref[0, :, gw:2 * gw].astype(F32)
    lane = lax.broadcasted_iota(jnp.int32, q.shape, 1)
    first_half = (lane % HEAD_DIM) < (HEAD_DIM // 2)
    cos = cos_ref[...]
    sin = sin_ref[...]

    def rope(x):
        rot = jnp.where(first_half, pltpu.roll(x, gw - HEAD_DIM // 2, 1), pltpu.roll(x, HEAD_DIM // 2, 1))
        return x * cos + rot * sin

    q = rope(q)
    k = rope(k) * HEAD_DIM ** -0.5
    qs = (q * wst_ref[...]).astype(BF16)
    ke = (k * wend_ref[...]).astype(BF16)
    qb = q.astype(BF16)
    kb = k.astype(BF16)
    onorm = on_ref[...]
    for h in range(N_HEADS):
        hs = _head_slice(h)
        v = p_ref[0, :, 2 * gw + h * HEAD_DIM:2 * gw + (h + 1) * HEAD_DIM]
        scores = (_dot_nt(qb[:, hs], kb[:, hs]) * dmat_ref[h]).astype(BF16)
        s_h = state[h]
        o = _dot(scores, v) + _dot(qs[:, hs], s_h.astype(BF16))
        state[h] = gdec_ref[h] * s_h + _dot_tn(ke[:, hs], v)
        oc = o - jnp.mean(o, axis=-1, keepdims=True)
        o = oc * lax.rsqrt(jnp.mean(oc * oc, axis=-1, keepdims=True) + EPS) * onorm
        gate = p_ref[0, :, 3 * gw + h * HEAD_DIM:3 * gw + (h + 1) * HEAD_DIM].astype(F32)
        o_ref[0, :, hs] = (o * _silu(gate)).astype(o_ref.dtype)


def _ret(p, cos, sin, dmat, wst, wend, gdec, onorm, t):
    b, s, _ = p.shape
    gw = GROUP_WIDTH
    return pl.pallas_call(
        _ret_kernel,
        grid=(b, s // t),
        in_specs=[
            pl.BlockSpec((1, t, 4 * gw), lambda i, j: (i, j, 1)),
            pl.BlockSpec((t, gw), lambda i, j: (j, 0)),
            pl.BlockSpec((t, gw), lambda i, j: (j, 0)),
            pl.BlockSpec((N_HEADS, t, t), lambda i, j: (0, 0, 0)),
            pl.BlockSpec((t, gw), lambda i, j: (0, 0)),
            pl.BlockSpec((t, gw), lambda i, j: (0, 0)),
            pl.BlockSpec(memory_space=pltpu.SMEM),
            pl.BlockSpec((1, HEAD_DIM), lambda i, j: (0, 0)),
        ],
        out_specs=pl.BlockSpec((1, t, gw), lambda i, j: (i, j, 0)),
        out_shape=jax.ShapeDtypeStruct((b, s, gw), BF16),
        scratch_shapes=[pltpu.VMEM((N_HEADS, HEAD_DIM, HEAD_DIM), F32)],
        compiler_params=pltpu.CompilerParams(
            dimension_semantics=("parallel", "arbitrary"), vmem_limit_bytes=VMEM_LIMIT),
        name="retention",
    )(p, cos, sin, dmat, wst, wend, gdec, onorm)


def _fox_prep_kernel(p_ref, ps_ref, qn_ref, kn_ref, fb_ref, utri_ref,
                     q_out, k_out, v_out, c_out, carry):
    gw = GROUP_WIDTH

    @pl.when(pl.program_id(1) == 0)
    def _():
        carry[...] = jnp.zeros_like(carry)

    qn = qn_ref[...]
    kn = kn_ref[...]
    ones_col = (lax.broadcasted_iota(jnp.int32, (p_ref.shape[1], HEAD_DIM), 1) == 0).astype(BF16)
    for h in range(N_HEADS):
        q = p_ref[0, :, h * HEAD_DIM:(h + 1) * HEAD_DIM].astype(F32)
        k = p_ref[0, :, gw + h * HEAD_DIM:gw + (h + 1) * HEAD_DIM].astype(F32)
        q = q * lax.rsqrt(jnp.mean(q * q, axis=-1, keepdims=True) + EPS) * qn * HEAD_DIM ** -0.5
        k = k * lax.rsqrt(jnp.mean(k * k, axis=-1, keepdims=True) + EPS) * kn
        q_out[0, h] = q.astype(BF16)
        k_out[0, h] = k.astype(BF16)
        v_out[0, h, :, 0:HEAD_DIM] = p_ref[0, :, 2 * gw + h * HEAD_DIM:2 * gw + (h + 1) * HEAD_DIM]
        v_out[0, h, :, HEAD_DIM:2 * HEAD_DIM] = ones_col

    x = ps_ref[0] + fb_ref[...]
    logf = jnp.minimum(x, 0.0) - jnp.log1p(jnp.exp(-jnp.abs(x)))
    logf_t = logf.T[8:16, :]
    c = _dot(logf_t, utri_ref[...], HI) + carry[:, 0:1]
    c_out[0, 0] = c
    t = c.shape[1]
    carry[...] = jnp.broadcast_to(c[:, t - 1:t], carry.shape)


def _fox_prep(p, ps, qn, kn, fb_row, utri, t):
    b, s, _ = p.shape
    gw = GROUP_WIDTH
    hm = jax.ShapeDtypeStruct((b, N_HEADS, s, HEAD_DIM), BF16)
    hm_spec = pl.BlockSpec((1, N_HEADS, t, HEAD_DIM), lambda i, j: (i, 0, j, 0))
    hv = jax.ShapeDtypeStruct((b, N_HEADS, s, 2 * HEAD_DIM), BF16)
    hv_spec = pl.BlockSpec((1, N_HEADS, t, 2 * HEAD_DIM), lambda i, j: (i, 0, j, 0))
    return pl.pallas_call(
        _fox_prep_kernel,
        grid=(b, s // t),
        in_specs=[
            pl.BlockSpec((1, t, 4 * gw), lambda i, j: (i, j, 2)),
            pl.BlockSpec((1, t, SMALL_W), lambda i, j: (i, j, 0)),
            pl.BlockSpec((1, HEAD_DIM), lambda i, j: (0, 0)),
            pl.BlockSpec((1, HEAD_DIM), lambda i, j: (0, 0)),
            pl.BlockSpec((1, SMALL_W), lambda i, j: (0, 0)),
            pl.BlockSpec((t, t), lambda i, j: (0, 0)),
        ],
        out_specs=[hm_spec, hm_spec, hv_spec, pl.BlockSpec((1, 1, 8, t), lambda i, j: (i, j, 0, 0))],
        out_shape=[hm, hm, hv, jax.ShapeDtypeStruct((b, s // t, 8, t), F32)],
        scratch_shapes=[pltpu.VMEM((8, 128), F32)],
        compiler_params=pltpu.CompilerParams(
            dimension_semantics=("parallel", "arbitrary"), vmem_limit_bytes=VMEM_LIMIT),
        name="fox_prep",
    )(p, ps, qn, kn, fb_row, utri)


FOX_STRIP = 32
LANES = 128


def _fox_kernel(q_ref, k_ref, v_ref, c_ref, g_ref, o_ref, m_s, acc_s, s_scr, p_scr, al_scr):
    qi = pl.program_id(1)
    tq = q_ref.shape[2]
    tk = c_ref.shape[3]
    nj = tk // LANES

    m_s[...] = jnp.full_like(m_s, NEG_BIG)
    acc_s[...] = jnp.zeros_like(acc_s)

    def update(ki, masked, slot):
        kr = pl.ds(pl.multiple_of(ki * tk, tk), tk)
        for h in range(N_HEADS):
            s_scr[slot, h] = _dot_nt(q_ref[0, h], k_ref[0, h, kr, :])
        row = lax.broadcasted_iota(jnp.int32, (FOX_STRIP, LANES), 0)
        col = lax.broadcasted_iota(jnp.int32, (FOX_STRIP, LANES), 1)
        for h in range(N_HEADS):
            c_blk = [c_ref[0, ki, h:h + 1, j * LANES:(j + 1) * LANES] for j in range(nj)]
            for i in range(tq // FOX_STRIP):
                r0 = i * FOX_STRIP
                r = slice(r0, r0 + FOX_STRIP)
                live = min(nj, (r0 + FOX_STRIP - 1) // LANES + 1) if masked else nj
                sb = [s_scr[slot, h, r, j * LANES:(j + 1) * LANES] - c_blk[j] for j in range(live)]
                if masked:
                    sb = [jnp.where(row + r0 >= col + j * LANES, sb[j], NEG_BIG)
                          if (j + 1) * LANES - 1 > r0 else sb[j] for j in range(live)]
                mx = sb[0]
                for j in range(1, live):
                    mx = jnp.maximum(mx, sb[j])
                m_old = m_s[h, r, :]
                m_new = jnp.maximum(m_old, jnp.max(mx, axis=-1, keepdims=True))
                m_s[h, r, :] = m_new
                al_scr[slot, h, r, :] = jnp.exp(m_old - m_new)
                for j in range(live):
                    p_scr[slot, h, r, j * LANES:(j + 1) * LANES] = jnp.exp(sb[j] - m_new).astype(BF16)
                for j in range(live, nj):
                    p_scr[slot, h, r, j * LANES:(j + 1) * LANES] = jnp.zeros((FOX_STRIP, LANES), BF16)
        for h in range(N_HEADS):
            acc_s[h] = al_scr[slot, h] * acc_s[h] + _dot(p_scr[slot, h], v_ref[0, h, kr, :])

    def below_diagonal_pair(pi, carry):
        update(2 * pi, False, 0)
        update(2 * pi + 1, False, 1)
        return carry

    lax.fori_loop(0, qi // 2, below_diagonal_pair, 0)

    @pl.when(qi % 2 == 1)
    def _():
        update(qi - 1, False, 0)

    update(qi, True, 1)
    for h in range(N_HEADS):
        hs = _head_slice(h)
        acc = acc_s[h]
        o = acc[:, 0:HEAD_DIM] / acc[:, HEAD_DIM:HEAD_DIM + 1]
        gate = g_ref[0, :, hs].astype(F32)
        o_ref[0, :, hs] = (o * _sigmoid(gate)).astype(o_ref.dtype)


def _fox(qh, kh, vh, c, p, t):
    b, _, s, _ = qh.shape
    gw = GROUP_WIDTH
    n = s // t
    return pl.pallas_call(
        _fox_kernel,
        grid=(b, n),
        in_specs=[
            pl.BlockSpec((1, N_HEADS, t, HEAD_DIM), lambda i, j: (i, 0, j, 0)),
            pl.BlockSpec((1, N_HEADS, s, HEAD_DIM), lambda i, j: (i, 0, 0, 0), pipeline_mode=pl.Buffered(1)),
            pl.BlockSpec((1, N_HEADS, s, 2 * HEAD_DIM), lambda i, j: (i, 0, 0, 0), pipeline_mode=pl.Buffered(1)),
            pl.BlockSpec((1, n, 8, t), lambda i, j: (i, 0, 0, 0), pipeline_mode=pl.Buffered(1)),
            pl.BlockSpec((1, t, gw), lambda i, j: (i, j, 4 * 2 + 3)),
        ],
        out_specs=pl.BlockSpec((1, t, gw), lambda i, j: (i, j, 0)),
        out_shape=jax.ShapeDtypeStruct((b, s, gw), BF16),
        scratch_shapes=[
            pltpu.VMEM((N_HEADS, t, LANES), F32),
            pltpu.VMEM((N_HEADS, t, 2 * HEAD_DIM), F32),
            pltpu.VMEM((2, N_HEADS, t, t), F32),
            pltpu.VMEM((2, N_HEADS, t, t), BF16),
            pltpu.VMEM((2, N_HEADS, t, LANES), F32),
        ],
        compiler_params=pltpu.CompilerParams(
            dimension_semantics=("parallel", "parallel"), vmem_limit_bytes=VMEM_LIMIT),
        name="fox_attn",
    )(qh, kh, vh, c, p)


N_LEVELS = 6


def _hgrn_tables():
    c = CHUNK
    mall = np.zeros((N_LEVELS + 2, c, c), np.float32)
    masks = np.zeros((N_LEVELS + 1, c, c), np.float32)
    for lv in range(N_LEVELS):
        half = 1 << lv
        for t in range(c):
            m = (t >> (lv + 1)) * (2 * half) + half
            if t >= m:
                mall[lv, t, m:t + 1] = 1.0
            else:
                mall[lv, t, t + 1:m] = 1.0
            for s in range(c):
                if (s >> (lv + 1)) == (t >> (lv + 1)) and t >= m and s < m:
                    masks[lv, t, s] = 1.0
    masks[N_LEVELS] = np.eye(c, dtype=np.float32)
    for t in range(c):
        mall[N_LEVELS, t, :t + 1] = 1.0
        mall[N_LEVELS + 1, t, t + 1:] = 1.0
    return mall.reshape((N_LEVELS + 2) * c, c), masks


def _hgrn_kernel(p_ref, lb_ref, on_ref, mall_ref, mask_ref, o_ref, state):
    t = p_ref.shape[1]
    gw = GROUP_WIDTH

    @pl.when(pl.program_id(1) == 0)
    def _():
        state[...] = jnp.zeros_like(state)

    lb = lb_ref[...]
    onorm = on_ref[...]
    mall = mall_ref[...]
    nl = N_LEVELS + 1
    lvl_mask = jnp.concatenate([mask_ref[...]] * N_HEADS, axis=0)

    def chunk_body(c, carry):
        r = pl.ds(pl.multiple_of(c * CHUNK, CHUNK), CHUNK)
        qx = p_ref[0, r, 0:gw].astype(F32)
        f = p_ref[0, r, gw:2 * gw].astype(F32)
        logf = jnp.log(lb + (1.0 - lb) * _sigmoid(f))
        kk = (1.0 - lb) * _sigmoid(-f)
        qq = _silu(qx)
        x_all = jnp.exp(_dot(mall, logf.astype(BF16)))
        x_q = x_all[N_LEVELS * CHUNK:(N_LEVELS + 1) * CHUNK, :]
        x_k = x_all[(N_LEVELS + 1) * CHUNK:(N_LEVELS + 2) * CHUNK, :]
        qb = qq.astype(BF16)
        kb = kk.astype(BF16)
        ql = [(qq * x_all[lv * CHUNK:(lv + 1) * CHUNK, :]).astype(BF16) for lv in range(N_LEVELS)] + [qb]
        kl = [(kk * x_all[lv * CHUNK:(lv + 1) * CHUNK, :]).astype(BF16) for lv in range(N_LEVELS)] + [kb]
        qc = (qq * x_q).astype(BF16)
        kc = (kk * x_k).astype(BF16)
        qs = jnp.stack([ql[lv][:, _head_slice(h)] for h in range(N_HEADS) for lv in range(nl)])
        ks = jnp.stack([kl[lv][:, _head_slice(h)] for h in range(N_HEADS) for lv in range(nl)])
        sc = _bdot_nt(qs, ks) * lvl_mask
        for h in range(N_HEADS):
            hs = _head_slice(h)
            scores = sc[h * nl]
            for lv in range(1, nl):
                scores = scores + sc[h * nl + lv]
            v = p_ref[0, r, 2 * gw + h * HEAD_DIM:2 * gw + (h + 1) * HEAD_DIM]
            st = state[h]
            o = _dot(scores.astype(BF16), v) + _dot_nt(qc[:, hs], st.astype(BF16))
            state[h] = st * x_q[CHUNK - 1:CHUNK, hs] + _dot_tn(v, kc[:, hs])
            o = o * lax.rsqrt(jnp.mean(o * o, axis=-1, keepdims=True) + EPS) * onorm
            gate = p_ref[0, r, 3 * gw + h * HEAD_DIM:3 * gw + (h + 1) * HEAD_DIM].astype(F32)
            o_ref[0, r, hs] = (o * _silu(gate)).astype(o_ref.dtype)
        return carry

    lax.fori_loop(0, t // CHUNK, chunk_body, 0, unroll=2)


def _hgrn(p, lb_row, onorm, mall, masks, t=512):
    b, s, _ = p.shape
    gw = GROUP_WIDTH
    return pl.pallas_call(
        _hgrn_kernel,
        grid=(b, s // t),
        in_specs=[
            pl.BlockSpec((1, t, 4 * gw), lambda i, j: (i, j, 3)),
            pl.BlockSpec((1, gw), lambda i, j: (0, 0)),
            pl.BlockSpec((1, HEAD_DIM), lambda i, j: (0, 0)),
            pl.BlockSpec(mall.shape, lambda i, j: (0, 0)),
            pl.BlockSpec(masks.shape, lambda i, j: (0, 0, 0)),
        ],
        out_specs=pl.BlockSpec((1, t, gw), lambda i, j: (i, j, 0)),
        out_shape=jax.ShapeDtypeStruct((b, s, gw), BF16),
        scratch_shapes=[pltpu.VMEM((N_HEADS, HEAD_DIM, HEAD_DIM), F32)],
        compiler_params=pltpu.CompilerParams(
            dimension_semantics=("parallel", "arbitrary"), vmem_limit_bytes=VMEM_LIMIT),
        name="hgrn2",
    )(p, lb_row, onorm, mall, masks)


def _post_kernel(h_ref, ya_ref, yb_ref, yc_ref, yd_ref, wo_ref, g_ref, wg_ref, wv_ref,
                 cg_ref, cv_ref, wd_ref, o_ref, buf_g, buf_v, un_s, act_s, *, tf):
    tm = h_ref.shape[1]
    gw = GROUP_WIDTH
    d_ff = wg_ref.shape[1]

    @pl.when(pl.program_id(1) == 0)
    def _():
        buf_g[0:HALO, :] = jnp.zeros((HALO, d_ff), F32)
        buf_v[0:HALO, :] = jnp.zeros((HALO, d_ff), F32)

    h1 = h_ref[0]
    for g, y_ref in enumerate((ya_ref, yb_ref, yc_ref, yd_ref)):
        h1 = h1 + _dot(y_ref[0], wo_ref[g * gw:(g + 1) * gw, :])
    ms = jnp.mean(h1 * h1, axis=-1, keepdims=True)
    un_s[...] = (h1 * lax.rsqrt(ms + EPS) * g_ref[...]).astype(BF16)
    o_ref[0] = h1

    def conv(up, buf, cw, fs):
        buf[HALO:HALO + tm, fs] = up
        out = cw[FFN_CONV - 1:FFN_CONV, fs] * up
        for j in range(FFN_CONV - 1):
            off = HALO - (FFN_CONV - 1) + j
            out = out + cw[j:j + 1, fs] * buf[off:off + tm, fs]
        buf[0:HALO, fs] = up[tm - HALO:tm, :]
        return out

    n_f = d_ff // tf
    split = (n_f // 2) * tf
    for f in range(n_f):
        fs = slice(f * tf, (f + 1) * tf)
        cg = conv(_dot(un_s[...], wg_ref[:, fs]), buf_g, cg_ref, fs)
        cv = conv(_dot(un_s[...], wv_ref[:, fs]), buf_v, cv_ref, fs)
        act_s[:, fs] = (_silu(cg) * cv).astype(BF16)
        if (f + 1) * tf == split:
            o_ref[0] += _dot(act_s[:, 0:split], wd_ref[0:split, :])
    o_ref[0] += _dot(act_s[:, split:d_ff], wd_ref[split:d_ff, :])


def _post(h, ya, yb, yc, yd, wo, g, wg, wv, cg, cv, wd, tm=512, tf=256):
    b, s, d = h.shape
    gw = GROUP_WIDTH
    d_ff = wg.shape[1]
    const = lambda shape: pl.BlockSpec(shape, lambda i, j: (0,) * len(shape))
    y_spec = pl.BlockSpec((1, tm, gw), lambda i, j: (i, j, 0))
    return pl.pallas_call(
        partial(_post_kernel, tf=tf),
        grid=(b, s // tm),
        in_specs=[
            pl.BlockSpec((1, tm, d), lambda i, j: (i, j, 0)),
            y_spec, y_spec, y_spec, y_spec,
            const((N_GROUPS * gw, d)),
            const((1, d)),
            const((d, d_ff)),
            const((d, d_ff)),
            const((FFN_CONV, d_ff)),
            const((FFN_CONV, d_ff)),
            const((d_ff, d)),
        ],
        out_specs=pl.BlockSpec((1, tm, d), lambda i, j: (i, j, 0)),
        out_shape=jax.ShapeDtypeStruct((b, s, d), F32),
        scratch_shapes=[
            pltpu.VMEM((tm + HALO, d_ff), F32),
            pltpu.VMEM((tm + HALO, d_ff), F32),
            pltpu.VMEM((tm, d), BF16),
            pltpu.VMEM((tm, d_ff), BF16),
        ],
        compiler_params=pltpu.CompilerParams(
            dimension_semantics=("parallel", "arbitrary"), vmem_limit_bytes=VMEM_LIMIT),
        name="post_ffn",
    )(h, ya, yb, yc, yd, wo, g, wg, wv, cg, cv, wd)


def _retention_tables(seq, t):
    hd = HEAD_DIM
    inv_freq = ROPE_BASE ** (-jnp.arange(0, hd, 2, dtype=F32) / hd)
    ang = jnp.arange(seq, dtype=F32)[:, None] * inv_freq[None, :]
    cos, sin = jnp.cos(ang), jnp.sin(ang)
    cos_t = jnp.tile(jnp.concatenate([cos, cos], axis=-1), (1, N_HEADS))
    sin_t = jnp.tile(jnp.concatenate([-sin, sin], axis=-1), (1, N_HEADS))
    lgh = jnp.log1p(-jnp.exp2(-RET_DECAY_EXP - jnp.arange(N_HEADS, dtype=F32)))
    n = jnp.arange(t, dtype=F32)
    diff = n[:, None] - n[None, :]
    keep = diff >= 0
    dmat = jnp.where(keep[None], jnp.exp(jnp.where(keep, diff, 0.0)[None] * lgh[:, None, None]), 0.0)
    wst = jnp.repeat(jnp.exp((n + 1.0)[:, None] * lgh[None, :]), hd, axis=1)
    wend = jnp.repeat(jnp.exp((t - 1.0 - n)[:, None] * lgh[None, :]), hd, axis=1)
    gdec = jnp.exp(t * lgh)
    return cos_t, sin_t, dmat, wst, wend, gdec


def kernel(x, norm_mix, norm_ffn, w_in, conv_qkv_a, a_log_a, dt_bias_a, onorm_a, onorm_b, qnorm_c, knorm_c,
           fbias_c, lower_bound_d, onorm_d, w_out, w_up, conv_ffn, w_down):
    b, s, d = x.shape
    depth = w_in.shape[0]
    gw = GROUP_WIDTH
    nh = N_HEADS
    d_ff = w_down.shape[1]
    t_ret = 512
    t_fox = 512

    oa = 4 * gw
    ob = oa + 2 * nh
    oc = ob + 4 * gw
    od = oc + 4 * gw + nh
    w_main = jnp.concatenate(
        [w_in[:, :, 0:oa], w_in[:, :, ob:ob + 4 * gw], w_in[:, :, oc:oc + 4 * gw], w_in[:, :, od:od + 4 * gw]],
        axis=-1).astype(BF16)
    w_small = jnp.concatenate([w_in[:, :, oa:ob], w_in[:, :, oc + 4 * gw:od]], axis=-1)
    w_small = jnp.pad(w_small, ((0, 0), (0, 0), (0, SMALL_W - 3 * nh))).astype(BF16)
    w_out_b = w_out.astype(BF16)
    w_g = w_up[:, :, :d_ff].astype(BF16)
    w_v = w_up[:, :, d_ff:].astype(BF16)
    w_down_b = w_down.astype(BF16)

    def small_row(vals, off):
        return jnp.zeros((depth, 1, SMALL_W), F32).at[:, 0, off:off + nh].set(vals.astype(F32))

    alog_rows = small_row(a_log_a, nh)
    dtb_rows = small_row(dt_bias_a, nh)
    fb_rows = small_row(fbias_c, 2 * nh)

    lbs = jax.nn.softmax(lower_bound_d.astype(F32), axis=0)
    lbs = jnp.cumsum(lbs, axis=0) - lbs[0]

    ltri = jnp.asarray(np.tril(np.ones((CHUNK, CHUNK), np.float32))).astype(BF16)
    utri = jnp.asarray(np.triu(np.ones((CHUNK, CHUNK), np.float32))).astype(BF16)
    utri_fox = jnp.asarray(np.triu(np.ones((t_fox, t_fox), np.float32)))
    mall_np, masks_np = _hgrn_tables()
    mall = jnp.asarray(mall_np).astype(BF16)
    masks = jnp.asarray(masks_np)
    cos_t, sin_t, dmat, wst, wend, gdec = _retention_tables(s, t_ret)

    h = x.astype(F32)
    for l in range(depth):
        pm, ps = _inproj(h.reshape(b * s, d), norm_mix[l].reshape(1, d).astype(F32), w_main[l], w_small[l])
        pm = pm.reshape(b, s, 4 * 4 * gw)
        ps = ps.reshape(b, s, SMALL_W)
        ya = _gdn(pm, ps, conv_qkv_a[l].astype(F32), alog_rows[l], dtb_rows[l],
                  jnp.tile(onorm_a[l].reshape(1, HEAD_DIM).astype(F32), (1, nh)), ltri, utri)
        yb = _ret(pm, cos_t, sin_t, dmat, wst, wend, gdec, onorm_b[l].reshape(1, HEAD_DIM).astype(F32), t_ret)
        qh, kh, vh, c = _fox_prep(pm, ps, qnorm_c[l].reshape(1, HEAD_DIM).astype(F32),
                                  knorm_c[l].reshape(1, HEAD_DIM).astype(F32), fb_rows[l], utri_fox, t_fox)
        yc = _fox(qh, kh, vh, c, pm, t_fox)
        yd = _hgrn(pm, lbs[l].reshape(1, gw), onorm_d[l].reshape(1, HEAD_DIM).astype(F32), mall, masks)
        h = _post(h, ya, yb, yc, yd, w_out_b[l], norm_ffn[l].reshape(1, d).astype(F32), w_g[l], w_v[l],
                  conv_ffn[l][:, :d_ff].astype(F32), conv_ffn[l][:, d_ff:].astype(F32), w_down_b[l])
    return h.astype(x.dtype)
```

```python
from functools import partial

import numpy as np
import jax
import jax.numpy as jnp
from jax import lax
from jax.experimental import pallas as pl
from jax.experimental.pallas import tpu as pltpu

F32 = jnp.float32
BF16 = jnp.bfloat16
HI = lax.Precision.HIGHEST

N_GROUPS = 4
HEAD_DIM = 64
N_HEADS = 4
GROUP_WIDTH = N_HEADS * HEAD_DIM
SHORT_CONV = 4
FFN_CONV = 3
ROPE_BASE = 10000.0
RET_DECAY_EXP = 5.0
EPS = 1e-6
NEG_BIG = -1e30
CHUNK = 64
SMALL_W = 128
HALO = 8
VMEM_LIMIT = 56 * 1024 * 1024


def _dot(a, b, precision=None):
    return jnp.dot(a, b, preferred_element_type=F32, precision=precision)


def _dot_nt(a, b, precision=None):
    return lax.dot_general(a, b, (((1,), (1,)), ((), ())), preferred_element_type=F32, precision=precision)


def _dot_tn(a, b, precision=None):
    return lax.dot_general(a, b, (((0,), (0,)), ((), ())), preferred_element_type=F32, precision=precision)


def _sigmoid(x):
    return 1.0 / (1.0 + jnp.exp(-x))


def _silu(x):
    return x * _sigmoid(x)


def _softplus(x):
    return jnp.maximum(x, 0.0) + jnp.log1p(jnp.exp(-jnp.abs(x)))


def _head_slice(h):
    return slice(h * HEAD_DIM, (h + 1) * HEAD_DIM)


def _inproj_kernel(x_ref, g_ref, wm_ref, ws_ref, pm_ref, ps_ref, un_ref, *, tn):
    x = x_ref[...]
    ms = jnp.mean(x * x, axis=-1, keepdims=True)
    un_ref[...] = (x * lax.rsqrt(ms + EPS) * g_ref[...]).astype(BF16)
    ps_ref[...] = _dot(un_ref[...], ws_ref[...])
    for j in range(wm_ref.shape[1] // tn):
        ns = slice(j * tn, (j + 1) * tn)
        pm_ref[:, ns] = _dot(un_ref[...], wm_ref[:, ns]).astype(BF16)


def _inproj(h2d, g, wm, ws, tm=1024, tn=1024):
    m, d = h2d.shape
    n = wm.shape[1]
    return pl.pallas_call(
        partial(_inproj_kernel, tn=tn),
        grid=(m // tm,),
        in_specs=[
            pl.BlockSpec((tm, d), lambda i: (i, 0)),
            pl.BlockSpec((1, d), lambda i: (0, 0)),
            pl.BlockSpec((d, n), lambda i: (0, 0)),
            pl.BlockSpec((d, SMALL_W), lambda i: (0, 0)),
        ],
        out_specs=[
            pl.BlockSpec((tm, n), lambda i: (i, 0)),
            pl.BlockSpec((tm, SMALL_W), lambda i: (i, 0)),
        ],
        out_shape=[
            jax.ShapeDtypeStruct((m, n), BF16),
            jax.ShapeDtypeStruct((m, SMALL_W), F32),
        ],
        scratch_shapes=[pltpu.VMEM((tm, d), BF16)],
        compiler_params=pltpu.CompilerParams(
            dimension_semantics=("parallel",), vmem_limit_bytes=VMEM_LIMIT),
        name="inproj",
    )(h2d, g, wm, ws)


def _dot01(m, x):
    hi = x.astype(BF16)
    lo = (x - hi.astype(F32)).astype(BF16)
    return _dot(m, hi) + _dot(m, lo)


def _bdot(a, b):
    return lax.dot_general(a, b, (((2,), (1,)), ((0,), (0,))), preferred_element_type=F32)


def _bdot_nt(a, b):
    return lax.dot_general(a, b, (((2,), (2,)), ((0,), (0,))), preferred_element_type=F32)


def _bdot_tn(a, b):
    return lax.dot_general(a, b, (((1,), (1,)), ((0,), (0,))), preferred_element_type=F32)


GDN_GROUP = 4


def _dot01_right(x, m):
    hi = x.astype(BF16)
    lo = (x - hi.astype(F32)).astype(BF16)
    return _dot(hi, m) + _dot(lo, m)


def _gdn_kernel(p_ref, ps_ref, cw_ref, alog_ref, dtb_ref, on_ref, ltri_ref, utri_ref,
                o_ref, xbuf, q_s, k_s, v_s, la_s, be_s, g_s, qe_s, o0_s, p_s, n_s, state):
    t = p_ref.shape[1]
    gw = GROUP_WIDTH

    @pl.when(pl.program_id(1) == 0)
    def _():
        xbuf[0:HALO, :] = jnp.zeros((HALO, 3 * gw), F32)
        state[...] = jnp.zeros_like(state)
        p_s[...] = jnp.zeros_like(p_s)
        n_s[...] = jnp.zeros_like(n_s)

    x = p_ref[0, :, 0:3 * gw].astype(F32)
    xbuf[HALO:HALO + t, :] = x
    cw = cw_ref[...]
    y = cw[3:4, :] * x
    for j in range(SHORT_CONV - 1):
        off = HALO - (SHORT_CONV - 1) + j
        y = y + cw[j:j + 1, :] * xbuf[off:off + t, :]
    xbuf[0:HALO, :] = x[t - HALO:t, :]
    y = _silu(y)
    brow = lax.broadcasted_iota(jnp.int32, (gw, gw), 0) // HEAD_DIM
    bcol = lax.broadcasted_iota(jnp.int32, (gw, gw), 1) // HEAD_DIM
    same_head = brow == bcol
    head_ones = same_head.astype(BF16)
    q = y[:, 0:gw]
    k = y[:, gw:2 * gw]
    q_s[...] = q * lax.rsqrt(_dot01_right(q * q, head_ones) + EPS) * HEAD_DIM ** -0.5
    k_s[...] = k * lax.rsqrt(_dot01_right(k * k, head_ones) + EPS)
    v_s[...] = y[:, 2 * gw:3 * gw]

    small = ps_ref[0]
    la_s[...] = -jnp.exp(alog_ref[...]) * _softplus(small + dtb_ref[...])
    be_s[...] = _sigmoid(small)

    row = lax.broadcasted_iota(jnp.int32, (CHUNK, CHUNK), 0)
    col = lax.broadcasted_iota(jnp.int32, (CHUNK, CHUNK), 1)
    tril = row >= col
    strict = row > col
    eye = (row == col).astype(F32)
    ones8 = jnp.ones((8, CHUNK), BF16)
    ltri = ltri_ref[...]
    utri = utri_ref[...].astype(F32)

    def factor_body(ci, carry):
        rows, qs, ks, vs, betas, gcs, grs = [], [], [], [], [], [], []
        for g in range(GDN_GROUP):
            r = pl.ds(pl.multiple_of((ci * GDN_GROUP + g) * CHUNK, CHUNK), CHUNK)
            rows.append(r)
            la_c = la_s[r, :]
            gcol_all = _dot01(ltri, la_c)
            g_s[r, :] = gcol_all
            be_c = be_s[r, :]
            for h in range(N_HEADS):
                hs = _head_slice(h)
                qs.append(q_s[r, hs])
                ks.append(k_s[r, hs])
                vs.append(v_s[r, hs])
                betas.append(be_c[:, h:h + 1])
                gcs.append(gcol_all[:, N_HEADS + h:N_HEADS + h + 1])
                la_col = la_c[:, N_HEADS + h:N_HEADS + h + 1]
                grs.append(_dot01(ones8, la_col * utri)[0:1, :])
        q = jnp.stack(qs)
        k = jnp.stack(ks)
        v = jnp.stack(vs)
        beta = jnp.stack(betas)
        gc = jnp.stack(gcs)
        gr = jnp.stack(grs)
        gam = jnp.where(tril, jnp.exp(jnp.where(tril, gc - gr, 0.0)), 0.0)
        kb = k * beta
        kbf = k.astype(BF16)
        a_mat = jnp.where(strict, _bdot_nt(kb.astype(BF16), kbf) * gam, 0.0)
        tinv = eye - a_mat
        pw = a_mat
        for _ in range(5):
            pwb = pw.astype(BF16)
            pw = _bdot(pwb, pwb)
            tinv = tinv + _bdot(tinv.astype(BF16), pw.astype(BF16))
        tb = tinv.astype(BF16)
        eg = jnp.exp(gc)
        gend = gc[:, CHUNK - 1:CHUNK, :]
        u = _bdot(tb, (v * beta).astype(BF16))
        w = _bdot(tb, (kb * eg).astype(BF16)).astype(BF16)
        qk = (_bdot_nt(q.astype(BF16), kbf) * gam).astype(BF16)
        qg = q * eg
        kd = (k * jnp.exp(gend - gc)).astype(BF16)
        ub = u.astype(BF16)
        pmat = _bdot_tn(kd, w).astype(BF16)
        nmat = _bdot_tn(kd, ub)
        qe = (qg - _bdot(qk, w)).astype(BF16)
        o0 = _bdot(qk, ub)
        for g in range(GDN_GROUP):
            cidx = ci * GDN_GROUP + g
            for h in range(N_HEADS):
                i = g * N_HEADS + h
                hs = _head_slice(h)
                qe_s[rows[g], hs] = qe[i]
                o0_s[rows[g], hs] = o0[i]
                p_s[cidx, hs, hs] = pmat[i]
                n_s[cidx, hs, hs] = nmat[i]
        return carry

    lax.fori_loop(0, t // (CHUNK * GDN_GROUP), factor_body, 0)

    lane_head = lax.broadcasted_iota(jnp.int32, (1, gw), 1) // HEAD_DIM
    onorm = on_ref[...]

    def scan_body(c, carry):
        r = pl.ds(pl.multiple_of(c * CHUNK, CHUNK), CHUNK)
        s_f = state[...]
        s_b = s_f.astype(BF16)
        o = _dot(qe_s[r, :], s_b) + o0_s[r, :]
        gend = g_s[r, :][CHUNK - 1:CHUNK, :]
        ge_row = jnp.zeros((1, gw), F32)
        for h in range(N_HEADS):
            ge_row = jnp.where(lane_head == h, jnp.exp(gend[:, N_HEADS + h:N_HEADS + h + 1]), ge_row)
        state[...] = s_f * ge_row - _dot(p_s[c], s_b) + n_s[c]
        ms = _dot01_right(o * o, head_ones) * (1.0 / HEAD_DIM)
        gate = p_ref[0, r, 3 * gw:4 * gw].astype(F32)
        o_ref[0, r, :] = (o * lax.rsqrt(ms + EPS) * onorm * _silu(gate)).astype(o_ref.dtype)
        return carry

    lax.fori_loop(0, t // CHUNK, scan_body, 0, unroll=2)


def _gdn(p, ps, cw, alog_row, dtb_row, onorm_t, ltri, utri, t=512):
    b, s, _ = p.shape
    gw = GROUP_WIDTH
    const = lambda shape: pl.BlockSpec(shape, lambda i, j: (0,) * len(shape))
    return pl.pallas_call(
        _gdn_kernel,
        grid=(b, s // t),
        in_specs=[
            pl.BlockSpec((1, t, 4 * gw), lambda i, j: (i, j, 0)),
            pl.BlockSpec((1, t, SMALL_W), lambda i, j: (i, j, 0)),
            const((SHORT_CONV, 3 * gw)),
            const((1, SMALL_W)),
            const((1, SMALL_W)),
            const((1, gw)),
            const((CHUNK, CHUNK)),
            const((CHUNK, CHUNK)),
        ],
        out_specs=pl.BlockSpec((1, t, gw), lambda i, j: (i, j, 0)),
        out_shape=jax.ShapeDtypeStruct((b, s, gw), BF16),
        scratch_shapes=[
            pltpu.VMEM((t + HALO, 3 * gw), F32),
            pltpu.VMEM((t, gw), F32),
            pltpu.VMEM((t, gw), F32),
            pltpu.VMEM((t, gw), F32),
            pltpu.VMEM((t, SMALL_W), F32),
            pltpu.VMEM((t, SMALL_W), F32),
            pltpu.VMEM((t, SMALL_W), F32),
            pltpu.VMEM((t, gw), BF16),
            pltpu.VMEM((t, gw), F32),
            pltpu.VMEM((t // CHUNK, gw, gw), BF16),
            pltpu.VMEM((t // CHUNK, gw, gw), F32),
            pltpu.VMEM((gw, gw), F32),
        ],
        compiler_params=pltpu.CompilerParams(
            dimension_semantics=("parallel", "arbitrary"), vmem_limit_bytes=VMEM_LIMIT),
        name="gdn",
    )(p, ps, cw, alog_row, dtb_row, onorm_t, ltri, utri)


def _ret_kernel(p_ref, cos_ref, sin_ref, dmat_ref, wst_ref, wend_ref, gdec_ref, on_ref,
                o_ref, state):
    gw = GROUP_WIDTH

    @pl.when(pl.program_id(1) == 0)
    def _():
        state[...] = jnp.zeros_like(state)

    q = p_ref[0, :, 0:gw].astype(F32)
    k = p_ref[0, :, gw:2 * gw].astype(F32)
    lane = lax.broadcasted_iota(jnp.int32, q.shape, 1)
    first_half = (lane % HEAD_DIM) < (HEAD_DIM // 2)
    cos = cos_ref[...]
    sin = sin_ref[...]

    def rope(x):
        rot = jnp.where(first_half, pltpu.roll(x, gw - HEAD_DIM // 2, 1), pltpu.roll(x, HEAD_DIM // 2, 1))
        return x * cos + rot * sin

    q = rope(q)
    k = rope(k) * HEAD_DIM ** -0.5
    qs = (q * wst_ref[...]).astype(BF16)
    ke = (k * wend_ref[...]).astype(BF16)
    qb = q.astype(BF16)
    kb = k.astype(BF16)
    onorm = on_ref[...]
    for h in range(N_HEADS):
        hs = _head_slice(h)
        v = p_ref[0, :, 2 * gw + h * HEAD_DIM:2 * gw + (h + 1) * HEAD_DIM]
        scores = (_dot_nt(qb[:, hs], kb[:, hs]) * dmat_ref[h]).astype(BF16)
        s_h = state[h]
        o = _dot(scores, v) + _dot(qs[:, hs], s_h.astype(BF16))
        state[h] = gdec_ref[h] * s_h + _dot_tn(ke[:, hs], v)
        oc = o - jnp.mean(o, axis=-1, keepdims=True)
        o = oc * lax.rsqrt(jnp.mean(oc * oc, axis=-1, keepdims=True) + EPS) * onorm
        gate = p_ref[0, :, 3 * gw + h * HEAD_DIM:3 * gw + (h + 1) * HEAD_DIM].astype(F32)
        o_ref[0, :, hs] = (o * _silu(gate)).astype(o_ref.dtype)


def _ret(p, cos, sin, dmat, wst, wend, gdec, onorm, t):
    b, s, _ = p.shape
    gw = GROUP_WIDTH
    return pl.pallas_call(
        _ret_kernel,
        grid=(b, s // t),
        in_specs=[
            pl.BlockSpec((1, t, 4 * gw), lambda i, j: (i, j, 1)),
            pl.BlockSpec((t, gw), lambda i, j: (j, 0)),
            pl.BlockSpec((t, gw), lambda i, j: (j, 0)),
            pl.BlockSpec((N_HEADS, t, t), lambda i, j: (0, 0, 0)),
            pl.BlockSpec((t, gw), lambda i, j: (0, 0)),
            pl.BlockSpec((t, gw), lambda i, j: (0, 0)),
            pl.BlockSpec(memory_space=pltpu.SMEM),
            pl.BlockSpec((1, HEAD_DIM), lambda i, j: (0, 0)),
        ],
        out_specs=pl.BlockSpec((1, t, gw), lambda i, j: (i, j, 0)),
        out_shape=jax.ShapeDtypeStruct((b, s, gw), BF16),
        scratch_shapes=[pltpu.VMEM((N_HEADS, HEAD_DIM, HEAD_DIM), F32)],
        compiler_params=pltpu.CompilerParams(
            dimension_semantics=("parallel", "arbitrary"), vmem_limit_bytes=VMEM_LIMIT),
        name="retention",
    )(p, cos, sin, dmat, wst, wend, gdec, onorm)


def _fox_prep_kernel(p_ref, ps_ref, qn_ref, kn_ref, fb_ref, utri_ref,
                     q_out, k_out, v_out, c_out, carry):
    gw = GROUP_WIDTH

    @pl.when(pl.program_id(1) == 0)
    def _():
        carry[...] = jnp.zeros_like(carry)

    brow = lax.broadcasted_iota(jnp.int32, (gw, gw), 0) // HEAD_DIM
    bcol = lax.broadcasted_iota(jnp.int32, (gw, gw), 1) // HEAD_DIM
    head_ones = (brow == bcol).astype(BF16)
    q = p_ref[0, :, 0:gw].astype(F32)
    k = p_ref[0, :, gw:2 * gw].astype(F32)
    q = q * lax.rsqrt(_dot01_right(q * q, head_ones) * (1.0 / HEAD_DIM) + EPS) * qn_ref[...] * HEAD_DIM ** -0.5
    k = k * lax.rsqrt(_dot01_right(k * k, head_ones) * (1.0 / HEAD_DIM) + EPS) * kn_ref[...]
    qb = q.astype(BF16)
    kb = k.astype(BF16)
    ones_col = (lax.broadcasted_iota(jnp.int32, (p_ref.shape[1], HEAD_DIM), 1) == 0).astype(BF16)
    for h in range(N_HEADS):
        q_out[0, h] = qb[:, _head_slice(h)]
        k_out[0, h] = kb[:, _head_slice(h)]
        v_out[0, h, :, 0:HEAD_DIM] = p_ref[0, :, 2 * gw + h * HEAD_DIM:2 * gw + (h + 1) * HEAD_DIM]
        v_out[0, h, :, HEAD_DIM:2 * HEAD_DIM] = ones_col

    x = ps_ref[0] + fb_ref[...]
    logf = jnp.minimum(x, 0.0) - jnp.log1p(jnp.exp(-jnp.abs(x)))
    logf_t = logf.T[8:16, :]
    c = _dot(logf_t, utri_ref[...], HI) + carry[:, 0:1]
    c_out[0, 0] = c
    t = c.shape[1]
    carry[...] = jnp.broadcast_to(c[:, t - 1:t], carry.shape)


def _fox_prep(p, ps, qn, kn, fb_row, utri, t):
    b, s, _ = p.shape
    gw = GROUP_WIDTH
    hm = jax.ShapeDtypeStruct((b, N_HEADS, s, HEAD_DIM), BF16)
    hm_spec = pl.BlockSpec((1, N_HEADS, t, HEAD_DIM), lambda i, j: (i, 0, j, 0))
    hv = jax.ShapeDtypeStruct((b, N_HEADS, s, 2 * HEAD_DIM), BF16)
    hv_spec = pl.BlockSpec((1, N_HEADS, t, 2 * HEAD_DIM), lambda i, j: (i, 0, j, 0))
    return pl.pallas_call(
        _fox_prep_kernel,
        grid=(b, s // t),
        in_specs=[
            pl.BlockSpec((1, t, 4 * gw), lambda i, j: (i, j, 2)),
            pl.BlockSpec((1, t, SMALL_W), lambda i, j: (i, j, 0)),
            pl.BlockSpec((1, gw), lambda i, j: (0, 0)),
            pl.BlockSpec((1, gw), lambda i, j: (0, 0)),
            pl.BlockSpec((1, SMALL_W), lambda i, j: (0, 0)),
            pl.BlockSpec((t, t), lambda i, j: (0, 0)),
        ],
        out_specs=[hm_spec, hm_spec, hv_spec, pl.BlockSpec((1, 1, 8, t), lambda i, j: (i, j, 0, 0))],
        out_shape=[hm, hm, hv, jax.ShapeDtypeStruct((b, s // t, 8, t), F32)],
        scratch_shapes=[pltpu.VMEM((8, 128), F32)],
        compiler_params=pltpu.CompilerParams(
            dimension_semantics=("parallel", "arbitrary"), vmem_limit_bytes=VMEM_LIMIT),
        name="fox_prep",
    )(p, ps, qn, kn, fb_row, utri)


FOX_STRIP = 32
LANES = 128


def _fox_kernel(q_ref, k_ref, v_ref, c_ref, g_ref, o_ref, m_s, acc_s, s_scr, p_scr, al_scr):
    qi = pl.program_id(1)
    tq = q_ref.shape[2]
    tk = c_ref.shape[3]
    nj = tk // LANES

    m_s[...] = jnp.full_like(m_s, NEG_BIG)
    acc_s[...] = jnp.zeros_like(acc_s)

    def update(ki, masked, slot):
        kr = pl.ds(pl.multiple_of(ki * tk, tk), tk)
        for h in range(N_HEADS):
            s_scr[slot, h] = _dot_nt(q_ref[0, h], k_ref[0, h, kr, :])
        row = lax.broadcasted_iota(jnp.int32, (FOX_STRIP, LANES), 0)
        col = lax.broadcasted_iota(jnp.int32, (FOX_STRIP, LANES), 1)
        for h in range(N_HEADS):
            c_blk = [c_ref[0, ki, h:h + 1, j * LANES:(j + 1) * LANES] for j in range(nj)]
            for i in range(tq // FOX_STRIP):
                r0 = i * FOX_STRIP
                r = slice(r0, r0 + FOX_STRIP)
                live = min(nj, (r0 + FOX_STRIP - 1) // LANES + 1) if masked else nj
                sb = [s_scr[slot, h, r, j * LANES:(j + 1) * LANES] - c_blk[j] for j in range(live)]
                if masked:
                    sb = [jnp.where(row + r0 >= col + j * LANES, sb[j], NEG_BIG)
                          if (j + 1) * LANES - 1 > r0 else sb[j] for j in range(live)]
                mx = sb[0]
                for j in range(1, live):
                    mx = jnp.maximum(mx, sb[j])
                m_old = m_s[h, r, :]
                m_new = jnp.maximum(m_old, jnp.max(mx, axis=-1, keepdims=True))
                m_s[h, r, :] = m_new
                al_scr[slot, h, r, :] = jnp.exp(m_old - m_new)
                for j in range(live):
                    p_scr[slot, h, r, j * LANES:(j + 1) * LANES] = jnp.exp(sb[j] - m_new).astype(BF16)
                for j in range(live, nj):
                    p_scr[slot, h, r, j * LANES:(j + 1) * LANES] = jnp.zeros((FOX_STRIP, LANES), BF16)
        for h in range(N_HEADS):
            acc_s[h] = al_scr[slot, h] * acc_s[h] + _dot(p_scr[slot, h], v_ref[0, h, kr, :])

    def below_diagonal_pair(pi, carry):
        update(2 * pi, False, 0)
        update(2 * pi + 1, False, 1)
        return carry

    lax.fori_loop(0, qi // 2, below_diagonal_pair, 0)

    @pl.when(qi % 2 == 1)
    def _():
        update(qi - 1, False, 0)

    update(qi, True, 1)
    for h in range(N_HEADS):
        hs = _head_slice(h)
        acc = acc_s[h]
        o = acc[:, 0:HEAD_DIM] / acc[:, HEAD_DIM:HEAD_DIM + 1]
        gate = g_ref[0, :, hs].astype(F32)
        o_ref[0, :, hs] = (o * _sigmoid(gate)).astype(o_ref.dtype)


def _fox(qh, kh, vh, c, p, t):
    b, _, s, _ = qh.shape
    gw = GROUP_WIDTH
    n = s // t
    return pl.pallas_call(
        _fox_kernel,
        grid=(b, n),
        in_specs=[
            pl.BlockSpec((1, N_HEADS, t, HEAD_DIM), lambda i, j: (i, 0, j, 0)),
            pl.BlockSpec((1, N_HEADS, s, HEAD_DIM), lambda i, j: (i, 0, 0, 0), pipeline_mode=pl.Buffered(1)),
            pl.BlockSpec((1, N_HEADS, s, 2 * HEAD_DIM), lambda i, j: (i, 0, 0, 0), pipeline_mode=pl.Buffered(1)),
            pl.BlockSpec((1, n, 8, t), lambda i, j: (i, 0, 0, 0), pipeline_mode=pl.Buffered(1)),
            pl.BlockSpec((1, t, gw), lambda i, j: (i, j, 4 * 2 + 3)),
        ],
        out_specs=pl.BlockSpec((1, t, gw), lambda i, j: (i, j, 0)),
        out_shape=jax.ShapeDtypeStruct((b, s, gw), BF16),
        scratch_shapes=[
            pltpu.VMEM((N_HEADS, t, LANES), F32),
            pltpu.VMEM((N_HEADS, t, 2 * HEAD_DIM), F32),
            pltpu.VMEM((2, N_HEADS, t, t), F32),
            pltpu.VMEM((2, N_HEADS, t, t), BF16),
            pltpu.VMEM((2, N_HEADS, t, LANES), F32),
        ],
        compiler_params=pltpu.CompilerParams(
            dimension_semantics=("parallel", "parallel"), vmem_limit_bytes=VMEM_LIMIT),
        name="fox_attn",
    )(qh, kh, vh, c, p)


N_LEVELS = 6


def _hgrn_tables():
    c = CHUNK
    mall = np.zeros((N_LEVELS + 2, c, c), np.float32)
    masks = np.zeros((N_LEVELS + 1, c, c), np.float32)
    for lv in range(N_LEVELS):
        half = 1 << lv
        for t in range(c):
            m = (t >> (lv + 1)) * (2 * half) + half
            if t >= m:
                mall[lv, t, m:t + 1] = 1.0
            else:
                mall[lv, t, t + 1:m] = 1.0
            for s in range(c):
                if (s >> (lv + 1)) == (t >> (lv + 1)) and t >= m and s < m:
                    masks[lv, t, s] = 1.0
    masks[N_LEVELS] = np.eye(c, dtype=np.float32)
    for t in range(c):
        mall[N_LEVELS, t, :t + 1] = 1.0
        mall[N_LEVELS + 1, t, t + 1:] = 1.0
    return mall.reshape((N_LEVELS + 2) * c, c), masks


def _hgrn_kernel(p_ref, lb_ref, on_ref, mall_ref, mask_ref, o_ref, state):
    t = p_ref.shape[1]
    gw = GROUP_WIDTH

    @pl.when(pl.program_id(1) == 0)
    def _():
        state[...] = jnp.zeros_like(state)

    lb = lb_ref[...]
    onorm = on_ref[...]
    mall = mall_ref[...]
    nl = N_LEVELS + 1
    lvl_mask = jnp.concatenate([mask_ref[...]] * N_HEADS, axis=0)

    def chunk_body(c, carry):
        r = pl.ds(pl.multiple_of(c * CHUNK, CHUNK), CHUNK)
        qx = p_ref[0, r, 0:gw].astype(F32)
        f = p_ref[0, r, gw:2 * gw].astype(F32)
        logf = jnp.log(lb + (1.0 - lb) * _sigmoid(f))
        kk = (1.0 - lb) * _sigmoid(-f)
        qq = _silu(qx)
        x_all = jnp.exp(_dot(mall, logf.astype(BF16)))
        x_q = x_all[N_LEVELS * CHUNK:(N_LEVELS + 1) * CHUNK, :]
        x_k = x_all[(N_LEVELS + 1) * CHUNK:(N_LEVELS + 2) * CHUNK, :]
        qb = qq.astype(BF16)
        kb = kk.astype(BF16)
        ql = [(qq * x_all[lv * CHUNK:(lv + 1) * CHUNK, :]).astype(BF16) for lv in range(N_LEVELS)] + [qb]
        kl = [(kk * x_all[lv * CHUNK:(lv + 1) * CHUNK, :]).astype(BF16) for lv in range(N_LEVELS)] + [kb]
        qc = (qq * x_q).astype(BF16)
        kc = (kk * x_k).astype(BF16)
        qs = jnp.stack([ql[lv][:, _head_slice(h)] for h in range(N_HEADS) for lv in range(nl)])
        ks = jnp.stack([kl[lv][:, _head_slice(h)] for h in range(N_HEADS) for lv in range(nl)])
        sc = _bdot_nt(qs, ks) * lvl_mask
        for h in range(N_HEADS):
            hs = _head_slice(h)
            scores = sc[h * nl]
            for lv in range(1, nl):
                scores = scores + sc[h * nl + lv]
            v = p_ref[0, r, 2 * gw + h * HEAD_DIM:2 * gw + (h + 1) * HEAD_DIM]
            st = state[h]
            o = _dot(scores.astype(BF16), v) + _dot_nt(qc[:, hs], st.astype(BF16))
            state[h] = st * x_q[CHUNK - 1:CHUNK, hs] + _dot_tn(v, kc[:, hs])
            o = o * lax.rsqrt(jnp.mean(o * o, axis=-1, keepdims=True) + EPS) * onorm
            gate = p_ref[0, r, 3 * gw + h * HEAD_DIM:3 * gw + (h + 1) * HEAD_DIM].astype(F32)
            o_ref[0, r, hs] = (o * _silu(gate)).astype(o_ref.dtype)
        return carry

    lax.fori_loop(0, t // CHUNK, chunk_body, 0, unroll=2)


def _hgrn(p, lb_row, onorm, mall, masks, t=512):
    b, s, _ = p.shape
    gw = GROUP_WIDTH
    return pl.pallas_call(
        _hgrn_kernel,
        grid=(b, s // t),
        in_specs=[
            pl.BlockSpec((1, t, 4 * gw), lambda i, j: (i, j, 3)),
            pl.BlockSpec((1, gw), lambda i, j: (0, 0)),
            pl.BlockSpec((1, HEAD_DIM), lambda i, j: (0, 0)),
            pl.BlockSpec(mall.shape, lambda i, j: (0, 0)),
            pl.BlockSpec(masks.shape, lambda i, j: (0, 0, 0)),
        ],
        out_specs=pl.BlockSpec((1, t, gw), lambda i, j: (i, j, 0)),
        out_shape=jax.ShapeDtypeStruct((b, s, gw), BF16),
        scratch_shapes=[pltpu.VMEM((N_HEADS, HEAD_DIM, HEAD_DIM), F32)],
        compiler_params=pltpu.CompilerParams(
            dimension_semantics=("parallel", "arbitrary"), vmem_limit_bytes=VMEM_LIMIT),
        name="hgrn2",
    )(p, lb_row, onorm, mall, masks)


def _post_kernel(h_ref, ya_ref, yb_ref, yc_ref, yd_ref, wo_ref, g_ref, wg_ref, wv_ref,
                 cg_ref, cv_ref, wd_ref, o_ref, buf_g, buf_v, un_s, act_s, *, tf):
    tm = h_ref.shape[1]
    gw = GROUP_WIDTH
    d_ff = wg_ref.shape[1]

    @pl.when(pl.program_id(1) == 0)
    def _():
        buf_g[0:HALO, :] = jnp.zeros((HALO, d_ff), F32)
        buf_v[0:HALO, :] = jnp.zeros((HALO, d_ff), F32)

    h1 = h_ref[0]
    for g, y_ref in enumerate((ya_ref, yb_ref, yc_ref, yd_ref)):
        h1 = h1 + _dot(y_ref[0], wo_ref[g * gw:(g + 1) * gw, :])
    ms = jnp.mean(h1 * h1, axis=-1, keepdims=True)
    un_s[...] = (h1 * lax.rsqrt(ms + EPS) * g_ref[...]).astype(BF16)
    o_ref[0] = h1

    def conv(up, buf, cw, fs):
        buf[HALO:HALO + tm, fs] = up
        out = cw[FFN_CONV - 1:FFN_CONV, fs] * up
        for j in range(FFN_CONV - 1):
            off = HALO - (FFN_CONV - 1) + j
            out = out + cw[j:j + 1, fs] * buf[off:off + tm, fs]
        buf[0:HALO, fs] = up[tm - HALO:tm, :]
        return out

    n_f = d_ff // tf
    split = (n_f // 2) * tf
    for f in range(n_f):
        fs = slice(f * tf, (f + 1) * tf)
        cg = conv(_dot(un_s[...], wg_ref[:, fs]), buf_g, cg_ref, fs)
        cv = conv(_dot(un_s[...], wv_ref[:, fs]), buf_v, cv_ref, fs)
        act_s[:, fs] = (_silu(cg) * cv).astype(BF16)
        if (f + 1) * tf == split:
            o_ref[0] += _dot(act_s[:, 0:split], wd_ref[0:split, :])
    o_ref[0] += _dot(act_s[:, split:d_ff], wd_ref[split:d_ff, :])


def _post(h, ya, yb, yc, yd, wo, g, wg, wv, cg, cv, wd, tm=512, tf=256):
    b, s, d = h.shape
    gw = GROUP_WIDTH
    d_ff = wg.shape[1]
    const = lambda shape: pl.BlockSpec(shape, lambda i, j: (0,) * len(shape))
    y_spec = pl.BlockSpec((1, tm, gw), lambda i, j: (i, j, 0))
    return pl.pallas_call(
        partial(_post_kernel, tf=tf),
        grid=(b, s // tm),
        in_specs=[
            pl.BlockSpec((1, tm, d), lambda i, j: (i, j, 0)),
            y_spec, y_spec, y_spec, y_spec,
            const((N_GROUPS * gw, d)),
            const((1, d)),
            const((d, d_ff)),
            const((d, d_ff)),
            const((FFN_CONV, d_ff)),
            const((FFN_CONV, d_ff)),
            const((d_ff, d)),
        ],
        out_specs=pl.BlockSpec((1, tm, d), lambda i, j: (i, j, 0)),
        out_shape=jax.ShapeDtypeStruct((b, s, d), F32),
        scratch_shapes=[
            pltpu.VMEM((tm + HALO, d_ff), F32),
            pltpu.VMEM((tm + HALO, d_ff), F32),
            pltpu.VMEM((tm, d), BF16),
            pltpu.VMEM((tm, d_ff), BF16),
        ],
        compiler_params=pltpu.CompilerParams(
            dimension_semantics=("parallel", "arbitrary"), vmem_limit_bytes=VMEM_LIMIT),
        name="post_ffn",
    )(h, ya, yb, yc, yd, wo, g, wg, wv, cg, cv, wd)


def _retention_tables(seq, t):
    hd = HEAD_DIM
    inv_freq = ROPE_BASE ** (-jnp.arange(0, hd, 2, dtype=F32) / hd)
    ang = jnp.arange(seq, dtype=F32)[:, None] * inv_freq[None, :]
    cos, sin = jnp.cos(ang), jnp.sin(ang)
    cos_t = jnp.tile(jnp.concatenate([cos, cos], axis=-1), (1, N_HEADS))
    sin_t = jnp.tile(jnp.concatenate([-sin, sin], axis=-1), (1, N_HEADS))
    lgh = jnp.log1p(-jnp.exp2(-RET_DECAY_EXP - jnp.arange(N_HEADS, dtype=F32)))
    n = jnp.arange(t, dtype=F32)
    diff = n[:, None] - n[None, :]
    keep = diff >= 0
    dmat = jnp.where(keep[None], jnp.exp(jnp.where(keep, diff, 0.0)[None] * lgh[:, None, None]), 0.0)
    wst = jnp.repeat(jnp.exp((n + 1.0)[:, None] * lgh[None, :]), hd, axis=1)
    wend = jnp.repeat(jnp.exp((t - 1.0 - n)[:, None] * lgh[None, :]), hd, axis=1)
    gdec = jnp.exp(t * lgh)
    return cos_t, sin_t, dmat, wst, wend, gdec


def kernel(x, norm_mix, norm_ffn, w_in, conv_qkv_a, a_log_a, dt_bias_a, onorm_a, onorm_b, qnorm_c, knorm_c,
           fbias_c, lower_bound_d, onorm_d, w_out, w_up, conv_ffn, w_down):
    b, s, d = x.shape
    depth = w_in.shape[0]
    gw = GROUP_WIDTH
    nh = N_HEADS
    d_ff = w_down.shape[1]
    t_ret = 512
    t_fox = 512

    oa = 4 * gw
    ob = oa + 2 * nh
    oc = ob + 4 * gw
    od = oc + 4 * gw + nh
    w_main = jnp.concatenate(
        [w_in[:, :, 0:oa], w_in[:, :, ob:ob + 4 * gw], w_in[:, :, oc:oc + 4 * gw], w_in[:, :, od:od + 4 * gw]],
        axis=-1).astype(BF16)
    w_small = jnp.concatenate([w_in[:, :, oa:ob], w_in[:, :, oc + 4 * gw:od]], axis=-1)
    w_small = jnp.pad(w_small, ((0, 0), (0, 0), (0, SMALL_W - 3 * nh))).astype(BF16)
    w_out_b = w_out.astype(BF16)
    w_g = w_up[:, :, :d_ff].astype(BF16)
    w_v = w_up[:, :, d_ff:].astype(BF16)
    w_down_b = w_down.astype(BF16)

    def small_row(vals, off):
        return jnp.zeros((depth, 1, SMALL_W), F32).at[:, 0, off:off + nh].set(vals.astype(F32))

    alog_rows = small_row(a_log_a, nh)
    dtb_rows = small_row(dt_bias_a, nh)
    fb_rows = small_row(fbias_c, 2 * nh)

    lbs = jax.nn.softmax(lower_bound_d.astype(F32), axis=0)
    lbs = jnp.cumsum(lbs, axis=0) - lbs[0]

    ltri = jnp.asarray(np.tril(np.ones((CHUNK, CHUNK), np.float32))).astype(BF16)
    utri = jnp.asarray(np.triu(np.ones((CHUNK, CHUNK), np.float32))).astype(BF16)
    utri_fox = jnp.asarray(np.triu(np.ones((t_fox, t_fox), np.float32)))
    mall_np, masks_np = _hgrn_tables()
    mall = jnp.asarray(mall_np).astype(BF16)
    masks = jnp.asarray(masks_np)
    cos_t, sin_t, dmat, wst, wend, gdec = _retention_tables(s, t_ret)

    h = x.astype(F32)
    for l in range(depth):
        pm, ps = _inproj(h.reshape(b * s, d), norm_mix[l].reshape(1, d).astype(F32), w_main[l], w_small[l])
        pm = pm.reshape(b, s, 4 * 4 * gw)
        ps = ps.reshape(b, s, SMALL_W)
        ya = _gdn(pm, ps, conv_qkv_a[l].astype(F32), alog_rows[l], dtb_rows[l],
                  jnp.tile(onorm_a[l].reshape(1, HEAD_DIM).astype(F32), (1, nh)), ltri, utri)
        yb = _ret(pm, cos_t, sin_t, dmat, wst, wend, gdec, onorm_b[l].reshape(1, HEAD_DIM).astype(F32), t_ret)
        qh, kh, vh, c = _fox_prep(pm, ps, jnp.tile(qnorm_c[l].reshape(1, HEAD_DIM).astype(F32), (1, nh)),
                                  jnp.tile(knorm_c[l].reshape(1, HEAD_DIM).astype(F32), (1, nh)),
                                  fb_rows[l], utri_fox, t_fox)
        yc = _fox(qh, kh, vh, c, pm, t_fox)
        yd = _hgrn(pm, lbs[l].reshape(1, gw), onorm_d[l].reshape(1, HEAD_DIM).astype(F32), mall, masks)
        h = _post(h, ya, yb, yc, yd, w_out_b[l], norm_ffn[l].reshape(1, d).astype(F32), w_g[l], w_v[l],
                  conv_ffn[l][:, :d_ff].astype(F32), conv_ffn[l][:, d_ff:].astype(F32), w_down_b[l])
    return h.astype(x.dtype)
```

```python
from functools import partial

import numpy as np
import jax
import jax.numpy as jnp
from jax import lax
from jax.experimental import pallas as pl
from jax.experimental.pallas import tpu as pltpu

F32 = jnp.float32
BF16 = jnp.bfloat16
HI = lax.Precision.HIGHEST

N_GROUPS = 4
HEAD_DIM = 64
N_HEADS = 4
GROUP_WIDTH = N_HEADS * HEAD_DIM
SHORT_CONV = 4
FFN_CONV = 3
ROPE_BASE = 10000.0
RET_DECAY_EXP = 5.0
EPS = 1e-6
NEG_BIG = -1e30
CHUNK = 64
SMALL_W = 128
HALO = 8
VMEM_LIMIT = 56 * 1024 * 1024


def _dot(a, b, precision=None):
    return jnp.dot(a, b, preferred_element_type=F32, precision=precision)


def _dot_nt(a, b, precision=None):
    return lax.dot_general(a, b, (((1,), (1,)), ((), ())), preferred_element_type=F32, precision=precision)


def _dot_tn(a, b, precision=None):
    return lax.dot_general(a, b, (((0,), (0,)), ((), ())), preferred_element_type=F32, precision=precision)


def _sigmoid(x):
    return 1.0 / (1.0 + jnp.exp(-x))


def _silu(x):
    return x * _sigmoid(x)


def _softplus(x):
    return jnp.maximum(x, 0.0) + jnp.log1p(jnp.exp(-jnp.abs(x)))


def _head_slice(h):
    return slice(h * HEAD_DIM, (h + 1) * HEAD_DIM)


def _inproj_kernel(x_ref, g_ref, wm_ref, ws_ref, pm_ref, ps_ref, un_ref, *, tn):
    x = x_ref[...]
    ms = jnp.mean(x * x, axis=-1, keepdims=True)
    un_ref[...] = (x * lax.rsqrt(ms + EPS) * g_ref[...]).astype(BF16)
    ps_ref[...] = _dot(un_ref[...], ws_ref[...])
    for j in range(wm_ref.shape[1] // tn):
        ns = slice(j * tn, (j + 1) * tn)
        pm_ref[:, ns] = _dot(un_ref[...], wm_ref[:, ns]).astype(BF16)


def _inproj(h2d, g, wm, ws, tm=1024, tn=1024):
    m, d = h2d.shape
    n = wm.shape[1]
    return pl.pallas_call(
        partial(_inproj_kernel, tn=tn),
        grid=(m // tm,),
        in_specs=[
            pl.BlockSpec((tm, d), lambda i: (i, 0)),
            pl.BlockSpec((1, d), lambda i: (0, 0)),
            pl.BlockSpec((d, n), lambda i: (0, 0)),
            pl.BlockSpec((d, SMALL_W), lambda i: (0, 0)),
        ],
        out_specs=[
            pl.BlockSpec((tm, n), lambda i: (i, 0)),
            pl.BlockSpec((tm, SMALL_W), lambda i: (i, 0)),
        ],
        out_shape=[
            jax.ShapeDtypeStruct((m, n), BF16),
            jax.ShapeDtypeStruct((m, SMALL_W), F32),
        ],
        scratch_shapes=[pltpu.VMEM((tm, d), BF16)],
        compiler_params=pltpu.CompilerParams(
            dimension_semantics=("parallel",), vmem_limit_bytes=VMEM_LIMIT),
        name="inproj",
    )(h2d, g, wm, ws)


def _dot01(m, x):
    hi = x.astype(BF16)
    lo = (x - hi.astype(F32)).astype(BF16)
    return _dot(m, hi) + _dot(m, lo)


def _bdot(a, b):
    return lax.dot_general(a, b, (((2,), (1,)), ((0,), (0,))), preferred_element_type=F32)


def _bdot_nt(a, b):
    return lax.dot_general(a, b, (((2,), (2,)), ((0,), (0,))), preferred_element_type=F32)


def _bdot_tn(a, b):
    return lax.dot_general(a, b, (((1,), (1,)), ((0,), (0,))), preferred_element_type=F32)


GDN_GROUP = 8


def _dot01_right(x, m):
    hi = x.astype(BF16)
    lo = (x - hi.astype(F32)).astype(BF16)
    return _dot(hi, m) + _dot(lo, m)


def _gdn_kernel(p_ref, ps_ref, cw_ref, alog_ref, dtb_ref, on_ref, ltri_ref, utri_ref,
                o_ref, xbuf, q_s, k_s, v_s, la_s, be_s, g_s, qe_s, o0_s, p_s, n_s, state):
    t = p_ref.shape[1]
    gw = GROUP_WIDTH

    @pl.when(pl.program_id(1) == 0)
    def _():
        xbuf[0:HALO, :] = jnp.zeros((HALO, 3 * gw), F32)
        state[...] = jnp.zeros_like(state)
        p_s[...] = jnp.zeros_like(p_s)
        n_s[...] = jnp.zeros_like(n_s)

    x = p_ref[0, :, 0:3 * gw].astype(F32)
    xbuf[HALO:HALO + t, :] = x
    cw = cw_ref[...]
    y = cw[3:4, :] * x
    for j in range(SHORT_CONV - 1):
        off = HALO - (SHORT_CONV - 1) + j
        y = y + cw[j:j + 1, :] * xbuf[off:off + t, :]
    xbuf[0:HALO, :] = x[t - HALO:t, :]
    y = _silu(y)
    brow = lax.broadcasted_iota(jnp.int32, (gw, gw), 0) // HEAD_DIM
    bcol = lax.broadcasted_iota(jnp.int32, (gw, gw), 1) // HEAD_DIM
    same_head = brow == bcol
    head_ones = same_head.astype(BF16)
    q = y[:, 0:gw]
    k = y[:, gw:2 * gw]
    q_s[...] = q * lax.rsqrt(_dot01_right(q * q, head_ones) + EPS) * HEAD_DIM ** -0.5
    k_s[...] = k * lax.rsqrt(_dot01_right(k * k, head_ones) + EPS)
    v_s[...] = y[:, 2 * gw:3 * gw]

    small = ps_ref[0]
    la_s[...] = -jnp.exp(alog_ref[...]) * _softplus(small + dtb_ref[...])
    be_s[...] = _sigmoid(small)

    row = lax.broadcasted_iota(jnp.int32, (CHUNK, CHUNK), 0)
    col = lax.broadcasted_iota(jnp.int32, (CHUNK, CHUNK), 1)
    tril = row >= col
    strict = row > col
    eye = (row == col).astype(F32)
    ones8 = jnp.ones((8, CHUNK), BF16)
    ltri = ltri_ref[...]
    utri = utri_ref[...].astype(F32)

    def factor_body(ci, carry):
        rows, qs, ks, vs, betas, gcs, grs = [], [], [], [], [], [], []
        for g in range(GDN_GROUP):
            r = pl.ds(pl.multiple_of((ci * GDN_GROUP + g) * CHUNK, CHUNK), CHUNK)
            rows.append(r)
            la_c = la_s[r, :]
            gcol_all = _dot01(ltri, la_c)
            g_s[r, :] = gcol_all
            be_c = be_s[r, :]
            for h in range(N_HEADS):
                hs = _head_slice(h)
                qs.append(q_s[r, hs])
                ks.append(k_s[r, hs])
                vs.append(v_s[r, hs])
                betas.append(be_c[:, h:h + 1])
                gcs.append(gcol_all[:, N_HEADS + h:N_HEADS + h + 1])
                la_col = la_c[:, N_HEADS + h:N_HEADS + h + 1]
                grs.append(_dot01(ones8, la_col * utri)[0:1, :])
        q = jnp.stack(qs)
        k = jnp.stack(ks)
        v = jnp.stack(vs)
        beta = jnp.stack(betas)
        gc = jnp.stack(gcs)
        gr = jnp.stack(grs)
        gam = jnp.where(tril, jnp.exp(jnp.where(tril, gc - gr, 0.0)), 0.0)
        kb = k * beta
        kbf = k.astype(BF16)
        a_mat = jnp.where(strict, _bdot_nt(kb.astype(BF16), kbf) * gam, 0.0)
        tinv = eye - a_mat
        pw = a_mat
        for _ in range(5):
            pwb = pw.astype(BF16)
            pw = _bdot(pwb, pwb)
            tinv = tinv + _bdot(tinv.astype(BF16), pw.astype(BF16))
        tb = tinv.astype(BF16)
        eg = jnp.exp(gc)
        gend = gc[:, CHUNK - 1:CHUNK, :]
        u = _bdot(tb, (v * beta).astype(BF16))
        w = _bdot(tb, (kb * eg).astype(BF16)).astype(BF16)
        qk = (_bdot_nt(q.astype(BF16), kbf) * gam).astype(BF16)
        qg = q * eg
        kd = (k * jnp.exp(gend - gc)).astype(BF16)
        ub = u.astype(BF16)
        pmat = _bdot_tn(kd, w).astype(BF16)
        nmat = _bdot_tn(kd, ub)
        qe = (qg - _bdot(qk, w)).astype(BF16)
        o0 = _bdot(qk, ub)
        for g in range(GDN_GROUP):
            cidx = ci * GDN_GROUP + g
            for h in range(N_HEADS):
                i = g * N_HEADS + h
                hs = _head_slice(h)
                qe_s[rows[g], hs] = qe[i]
                o0_s[rows[g], hs] = o0[i]
                p_s[cidx, hs, hs] = pmat[i]
                n_s[cidx, hs, hs] = nmat[i]
        return carry

    lax.fori_loop(0, t // (CHUNK * GDN_GROUP), factor_body, 0)

    lane_head = lax.broadcasted_iota(jnp.int32, (1, gw), 1) // HEAD_DIM
    onorm = on_ref[...]

    def scan_body(c, carry):
        r = pl.ds(pl.multiple_of(c * CHUNK, CHUNK), CHUNK)
        s_f = state[...]
        s_b = s_f.astype(BF16)
        o = _dot(qe_s[r, :], s_b) + o0_s[r, :]
        gend = g_s[r, :][CHUNK - 1:CHUNK, :]
        ge_row = jnp.zeros((1, gw), F32)
        for h in range(N_HEADS):
            ge_row = jnp.where(lane_head == h, jnp.exp(gend[:, N_HEADS + h:N_HEADS + h + 1]), ge_row)
        state[...] = s_f * ge_row - _dot(p_s[c], s_b) + n_s[c]
        ms = _dot01_right(o * o, head_ones) * (1.0 / HEAD_DIM)
        gate = p_ref[0, r, 3 * gw:4 * gw].astype(F32)
        o_ref[0, r, :] = (o * lax.rsqrt(ms + EPS) * onorm * _silu(gate)).astype(o_ref.dtype)
        return carry

    lax.fori_loop(0, t // CHUNK, scan_body, 0, unroll=4)


def _gdn(p, ps, cw, alog_row, dtb_row, onorm_t, ltri, utri, t=512):
    b, s, _ = p.shape
    gw = GROUP_WIDTH
    const = lambda shape: pl.BlockSpec(shape, lambda i, j: (0,) * len(shape))
    return pl.pallas_call(
        _gdn_kernel,
        grid=(b, s // t),
        in_specs=[
            pl.BlockSpec((1, t, 4 * gw), lambda i, j: (i, j, 0)),
            pl.BlockSpec((1, t, SMALL_W), lambda i, j: (i, j, 0)),
            const((SHORT_CONV, 3 * gw)),
            const((1, SMALL_W)),
            const((1, SMALL_W)),
            const((1, gw)),
            const((CHUNK, CHUNK)),
            const((CHUNK, CHUNK)),
        ],
        out_specs=pl.BlockSpec((1, t, gw), lambda i, j: (i, j, 0)),
        out_shape=jax.ShapeDtypeStruct((b, s, gw), BF16),
        scratch_shapes=[
            pltpu.VMEM((t + HALO, 3 * gw), F32),
            pltpu.VMEM((t, gw), F32),
            pltpu.VMEM((t, gw), F32),
            pltpu.VMEM((t, gw), F32),
            pltpu.VMEM((t, SMALL_W), F32),
            pltpu.VMEM((t, SMALL_W), F32),
            pltpu.VMEM((t, SMALL_W), F32),
            pltpu.VMEM((t, gw), BF16),
            pltpu.VMEM((t, gw), F32),
            pltpu.VMEM((t // CHUNK, gw, gw), BF16),
            pltpu.VMEM((t // CHUNK, gw, gw), F32),
            pltpu.VMEM((gw, gw), F32),
        ],
        compiler_params=pltpu.CompilerParams(
            dimension_semantics=("parallel", "arbitrary"), vmem_limit_bytes=VMEM_LIMIT),
        name="gdn",
    )(p, ps, cw, alog_row, dtb_row, onorm_t, ltri, utri)


def _ret_kernel(p_ref, cos_ref, sin_ref, dmat_ref, wst_ref, wend_ref, gdec_ref, on_ref,
                o_ref, state):
    gw = GROUP_WIDTH

    @pl.when(pl.program_id(1) == 0)
    def _():
        state[...] = jnp.zeros_like(state)

    q = p_ref[0, :, 0:gw].astype(F32)
    k = p_ref[0, :, gw:2 * gw].astype(F32)
    lane = lax.broadcasted_iota(jnp.int32, q.shape, 1)
    first_half = (lane % HEAD_DIM) < (HEAD_DIM // 2)
    cos = cos_ref[...]
    sin = sin_ref[...]

    def rope(x):
        rot = jnp.where(first_half, pltpu.roll(x, gw - HEAD_DIM // 2, 1), pltpu.roll(x, HEAD_DIM // 2, 1))
        return x * cos + rot * sin

    q = rope(q)
    k = rope(k) * HEAD_DIM ** -0.5
    qs = (q * wst_ref[...]).astype(BF16)
    ke = (k * wend_ref[...]).astype(BF16)
    qb = q.astype(BF16)
    kb = k.astype(BF16)
    onorm = on_ref[...]
    for h in range(N_HEADS):
        hs = _head_slice(h)
        v = p_ref[0, :, 2 * gw + h * HEAD_DIM:2 * gw + (h + 1) * HEAD_DIM]
        scores = (_dot_nt(qb[:, hs], kb[:, hs]) * dmat_ref[h]).astype(BF16)
        s_h = state[h]
        o = _dot(scores, v) + _dot(qs[:, hs], s_h.astype(BF16))
        state[h] = gdec_ref[h] * s_h + _dot_tn(ke[:, hs], v)
        oc = o - jnp.mean(o, axis=-1, keepdims=True)
        o = oc * lax.rsqrt(jnp.mean(oc * oc, axis=-1, keepdims=True) + EPS) * onorm
        gate = p_ref[0, :, 3 * gw + h * HEAD_DIM:3 * gw + (h + 1) * HEAD_DIM].astype(F32)
        o_ref[0, :, hs] = (o * _silu(gate)).astype(o_ref.dtype)


def _ret(p, cos, sin, dmat, wst, wend, gdec, onorm, t):
    b, s, _ = p.shape
    gw = GROUP_WIDTH
    return pl.pallas_call(
        _ret_kernel,
        grid=(b, s // t),
        in_specs=[
            pl.BlockSpec((1, t, 4 * gw), lambda i, j: (i, j, 1)),
            pl.BlockSpec((t, gw), lambda i, j: (j, 0)),
            pl.BlockSpec((t, gw), lambda i, j: (j, 0)),
            pl.BlockSpec((N_HEADS, t, t), lambda i, j: (0, 0, 0)),
            pl.BlockSpec((t, gw), lambda i, j: (0, 0)),
            pl.BlockSpec((t, gw), lambda i, j: (0, 0)),
            pl.BlockSpec(memory_space=pltpu.SMEM),
            pl.BlockSpec((1, HEAD_DIM), lambda i, j: (0, 0)),
        ],
        out_specs=pl.BlockSpec((1, t, gw), lambda i, j: (i, j, 0)),
        out_shape=jax.ShapeDtypeStruct((b, s, gw), BF16),
        scratch_shapes=[pltpu.VMEM((N_HEADS, HEAD_DIM, HEAD_DIM), F32)],
        compiler_params=pltpu.CompilerParams(
            dimension_semantics=("parallel", "arbitrary"), vmem_limit_bytes=VMEM_LIMIT),
        name="retention",
    )(p, cos, sin, dmat, wst, wend, gdec, onorm)


def _fox_prep_kernel(p_ref, ps_ref, qn_ref, kn_ref, fb_ref, utri_ref,
                     q_out, k_out, v_out, c_out, carry):
    gw = GROUP_WIDTH

    @pl.when(pl.program_id(1) == 0)
    def _():
        carry[...] = jnp.zeros_like(carry)

    brow = lax.broadcasted_iota(jnp.int32, (gw, gw), 0) // HEAD_DIM
    bcol = lax.broadcasted_iota(jnp.int32, (gw, gw), 1) // HEAD_DIM
    head_ones = (brow == bcol).astype(BF16)
    q = p_ref[0, :, 0:gw].astype(F32)
    k = p_ref[0, :, gw:2 * gw].astype(F32)
    q = q * lax.rsqrt(_dot01_right(q * q, head_ones) * (1.0 / HEAD_DIM) + EPS) * qn_ref[...] * HEAD_DIM ** -0.5
    k = k * lax.rsqrt(_dot01_right(k * k, head_ones) * (1.0 / HEAD_DIM) + EPS) * kn_ref[...]
    qb = q.astype(BF16)
    kb = k.astype(BF16)
    ones_col = (lax.broadcasted_iota(jnp.int32, (p_ref.shape[1], HEAD_DIM), 1) == 0).astype(BF16)
    for h in range(N_HEADS):
        q_out[0, h] = qb[:, _head_slice(h)]
        k_out[0, h] = kb[:, _head_slice(h)]
        v_out[0, h, :, 0:HEAD_DIM] = p_ref[0, :, 2 * gw + h * HEAD_DIM:2 * gw + (h + 1) * HEAD_DIM]
        v_out[0, h, :, HEAD_DIM:2 * HEAD_DIM] = ones_col

    x = ps_ref[0] + fb_ref[...]
    logf = jnp.minimum(x, 0.0) - jnp.log1p(jnp.exp(-jnp.abs(x)))
    logf_t = logf.T[8:16, :]
    c = _dot(logf_t, utri_ref[...], HI) + carry[:, 0:1]
    c_out[0, 0] = c
    t = c.shape[1]
    carry[...] = jnp.broadcast_to(c[:, t - 1:t], carry.shape)


def _fox_prep(p, ps, qn, kn, fb_row, utri, t):
    b, s, _ = p.shape
    gw = GROUP_WIDTH
    hm = jax.ShapeDtypeStruct((b, N_HEADS, s, HEAD_DIM), BF16)
    hm_spec = pl.BlockSpec((1, N_HEADS, t, HEAD_DIM), lambda i, j: (i, 0, j, 0))
    hv = jax.ShapeDtypeStruct((b, N_HEADS, s, 2 * HEAD_DIM), BF16)
    hv_spec = pl.BlockSpec((1, N_HEADS, t, 2 * HEAD_DIM), lambda i, j: (i, 0, j, 0))
    return pl.pallas_call(
        _fox_prep_kernel,
        grid=(b, s // t),
        in_specs=[
            pl.BlockSpec((1, t, 4 * gw), lambda i, j: (i, j, 2)),
            pl.BlockSpec((1, t, SMALL_W), lambda i, j: (i, j, 0)),
            pl.BlockSpec((1, gw), lambda i, j: (0, 0)),
            pl.BlockSpec((1, gw), lambda i, j: (0, 0)),
            pl.BlockSpec((1, SMALL_W), lambda i, j: (0, 0)),
            pl.BlockSpec((t, t), lambda i, j: (0, 0)),
        ],
        out_specs=[hm_spec, hm_spec, hv_spec, pl.BlockSpec((1, 1, 8, t), lambda i, j: (i, j, 0, 0))],
        out_shape=[hm, hm, hv, jax.ShapeDtypeStruct((b, s // t, 8, t), F32)],
        scratch_shapes=[pltpu.VMEM((8, 128), F32)],
        compiler_params=pltpu.CompilerParams(
            dimension_semantics=("parallel", "arbitrary"), vmem_limit_bytes=VMEM_LIMIT),
        name="fox_prep",
    )(p, ps, qn, kn, fb_row, utri)


FOX_STRIP = 32
LANES = 128


def _fox_kernel(q_ref, k_ref, v_ref, c_ref, g_ref, o_ref, m_s, acc_s, s_scr, p_scr, al_scr):
    qi = pl.program_id(1)
    tq = q_ref.shape[2]
    tk = c_ref.shape[3]
    nj = tk // LANES

    m_s[...] = jnp.full_like(m_s, NEG_BIG)
    acc_s[...] = jnp.zeros_like(acc_s)

    def update(ki, masked, slot):
        kr = pl.ds(pl.multiple_of(ki * tk, tk), tk)
        for h in range(N_HEADS):
            s_scr[slot, h] = _dot_nt(q_ref[0, h], k_ref[0, h, kr, :])
        row = lax.broadcasted_iota(jnp.int32, (FOX_STRIP, LANES), 0)
        col = lax.broadcasted_iota(jnp.int32, (FOX_STRIP, LANES), 1)
        for h in range(N_HEADS):
            c_blk = [c_ref[0, ki, h:h + 1, j * LANES:(j + 1) * LANES] for j in range(nj)]
            for i in range(tq // FOX_STRIP):
                r0 = i * FOX_STRIP
                r = slice(r0, r0 + FOX_STRIP)
                live = min(nj, (r0 + FOX_STRIP - 1) // LANES + 1) if masked else nj
                sb = [s_scr[slot, h, r, j * LANES:(j + 1) * LANES] - c_blk[j] for j in range(live)]
                if masked:
                    sb = [jnp.where(row + r0 >= col + j * LANES, sb[j], NEG_BIG)
                          if (j + 1) * LANES - 1 > r0 else sb[j] for j in range(live)]
                mx = sb[0]
                for j in range(1, live):
                    mx = jnp.maximum(mx, sb[j])
                m_old = m_s[h, r, :]
                m_new = jnp.maximum(m_old, jnp.max(mx, axis=-1, keepdims=True))
                m_s[h, r, :] = m_new
                al_scr[slot, h, r, :] = jnp.exp(m_old - m_new)
                for j in range(live):
                    p_scr[slot, h, r, j * LANES:(j + 1) * LANES] = jnp.exp(sb[j] - m_new).astype(BF16)
                for j in range(live, nj):
                    p_scr[slot, h, r, j * LANES:(j + 1) * LANES] = jnp.zeros((FOX_STRIP, LANES), BF16)
        for h in range(N_HEADS):
            acc_s[h] = al_scr[slot, h] * acc_s[h] + _dot(p_scr[slot, h], v_ref[0, h, kr, :])

    def below_diagonal_pair(pi, carry):
        update(2 * pi, False, 0)
        update(2 * pi + 1, False, 1)
        return carry

    lax.fori_loop(0, qi // 2, below_diagonal_pair, 0)

    @pl.when(qi % 2 == 1)
    def _():
        update(qi - 1, False, 0)

    update(qi, True, 1)
    for h in range(N_HEADS):
        hs = _head_slice(h)
        acc = acc_s[h]
        o = acc[:, 0:HEAD_DIM] / acc[:, HEAD_DIM:HEAD_DIM + 1]
        gate = g_ref[0, :, hs].astype(F32)
        o_ref[0, :, hs] = (o * _sigmoid(gate)).astype(o_ref.dtype)


def _fox(qh, kh, vh, c, p, t):
    b, _, s, _ = qh.shape
    gw = GROUP_WIDTH
    n = s // t
    return pl.pallas_call(
        _fox_kernel,
        grid=(b, n),
        in_specs=[
            pl.BlockSpec((1, N_HEADS, t, HEAD_DIM), lambda i, j: (i, 0, j, 0)),
            pl.BlockSpec((1, N_HEADS, s, HEAD_DIM), lambda i, j: (i, 0, 0, 0), pipeline_mode=pl.Buffered(1)),
            pl.BlockSpec((1, N_HEADS, s, 2 * HEAD_DIM), lambda i, j: (i, 0, 0, 0), pipeline_mode=pl.Buffered(1)),
            pl.BlockSpec((1, n, 8, t), lambda i, j: (i, 0, 0, 0), pipeline_mode=pl.Buffered(1)),
            pl.BlockSpec((1, t, gw), lambda i, j: (i, j, 4 * 2 + 3)),
        ],
        out_specs=pl.BlockSpec((1, t, gw), lambda i, j: (i, j, 0)),
        out_shape=jax.ShapeDtypeStruct((b, s, gw), BF16),
        scratch_shapes=[
            pltpu.VMEM((N_HEADS, t, LANES), F32),
            pltpu.VMEM((N_HEADS, t, 2 * HEAD_DIM), F32),
            pltpu.VMEM((2, N_HEADS, t, t), F32),
            pltpu.VMEM((2, N_HEADS, t, t), BF16),
            pltpu.VMEM((2, N_HEADS, t, LANES), F32),
        ],
        compiler_params=pltpu.CompilerParams(
            dimension_semantics=("parallel", "parallel"), vmem_limit_bytes=VMEM_LIMIT),
        name="fox_attn",
    )(qh, kh, vh, c, p)


N_LEVELS = 6


def _hgrn_tables():
    c = CHUNK
    mall = np.zeros((N_LEVELS + 2, c, c), np.float32)
    masks = np.zeros((N_LEVELS + 1, c, c), np.float32)
    for lv in range(N_LEVELS):
        half = 1 << lv
        for t in range(c):
            m = (t >> (lv + 1)) * (2 * half) + half
            if t >= m:
                mall[lv, t, m:t + 1] = 1.0
            else:
                mall[lv, t, t + 1:m] = 1.0
            for s in range(c):
                if (s >> (lv + 1)) == (t >> (lv + 1)) and t >= m and s < m:
                    masks[lv, t, s] = 1.0
    masks[N_LEVELS] = np.eye(c, dtype=np.float32)
    for t in range(c):
        mall[N_LEVELS, t, :t + 1] = 1.0
        mall[N_LEVELS + 1, t, t + 1:] = 1.0
    return mall.reshape((N_LEVELS + 2) * c, c), masks


def _hgrn_kernel(p_ref, lb_ref, on_ref, mall_ref, mask_ref, o_ref, state):
    t = p_ref.shape[1]
    gw = GROUP_WIDTH

    @pl.when(pl.program_id(1) == 0)
    def _():
        state[...] = jnp.zeros_like(state)

    lb = lb_ref[...]
    onorm = on_ref[...]
    mall = mall_ref[...]
    nl = N_LEVELS + 1
    brow = lax.broadcasted_iota(jnp.int32, (gw, gw), 0) // HEAD_DIM
    bcol = lax.broadcasted_iota(jnp.int32, (gw, gw), 1) // HEAD_DIM
    same_head = brow == bcol
    head_ones = same_head.astype(BF16)
    zero_b = jnp.zeros((), BF16)

    def per_head_rows(x):
        return jnp.where(same_head, jnp.concatenate([x] * N_HEADS, axis=0), zero_b)

    def chunk_body(c, carry):
        r = pl.ds(pl.multiple_of(c * CHUNK, CHUNK), CHUNK)
        qx = p_ref[0, r, 0:gw].astype(F32)
        f = p_ref[0, r, gw:2 * gw].astype(F32)
        logf = jnp.log(lb + (1.0 - lb) * _sigmoid(f))
        kk = (1.0 - lb) * _sigmoid(-f)
        qq = _silu(qx)
        x_all = jnp.exp(_dot(mall, logf.astype(BF16)))
        x_q = x_all[N_LEVELS * CHUNK:(N_LEVELS + 1) * CHUNK, :]
        x_k = x_all[(N_LEVELS + 1) * CHUNK:(N_LEVELS + 2) * CHUNK, :]
        scores = None
        for lv in range(nl):
            if lv < N_LEVELS:
                x_l = x_all[lv * CHUNK:(lv + 1) * CHUNK, :]
                q_l = (qq * x_l).astype(BF16)
                k_l = (kk * x_l).astype(BF16)
            else:
                q_l = qq.astype(BF16)
                k_l = kk.astype(BF16)
            term = _dot_nt(q_l, per_head_rows(k_l)) * mask_ref[lv]
            scores = term if scores is None else scores + term
        v = p_ref[0, r, 2 * gw:3 * gw]
        st = state[...]
        o = _dot(scores.astype(BF16), per_head_rows(v)) + _dot_nt((qq * x_q).astype(BF16), st.astype(BF16))
        state[...] = st * x_q[CHUNK - 1:CHUNK, :] + jnp.where(
            same_head, _dot_tn(v, (kk * x_k).astype(BF16)), 0.0)
        ms = _dot01_right(o * o, head_ones) * (1.0 / HEAD_DIM)
        gate = p_ref[0, r, 3 * gw:4 * gw].astype(F32)
        o_ref[0, r, :] = (o * lax.rsqrt(ms + EPS) * onorm * _silu(gate)).astype(o_ref.dtype)
        return carry

    lax.fori_loop(0, t // CHUNK, chunk_body, 0, unroll=8)


def _hgrn(p, lb_row, onorm, mall, masks, t=512):
    b, s, _ = p.shape
    gw = GROUP_WIDTH
    return pl.pallas_call(
        _hgrn_kernel,
        grid=(b, s // t),
        in_specs=[
            pl.BlockSpec((1, t, 4 * gw), lambda i, j: (i, j, 3)),
            pl.BlockSpec((1, gw), lambda i, j: (0, 0)),
            pl.BlockSpec((1, gw), lambda i, j: (0, 0)),
            pl.BlockSpec(mall.shape, lambda i, j: (0, 0)),
            pl.BlockSpec(masks.shape, lambda i, j: (0, 0, 0)),
        ],
        out_specs=pl.BlockSpec((1, t, gw), lambda i, j: (i, j, 0)),
        out_shape=jax.ShapeDtypeStruct((b, s, gw), BF16),
        scratch_shapes=[pltpu.VMEM((gw, gw), F32)],
        compiler_params=pltpu.CompilerParams(
            dimension_semantics=("parallel", "arbitrary"), vmem_limit_bytes=VMEM_LIMIT),
        name="hgrn2",
    )(p, lb_row, onorm, mall, masks)


def _post_kernel(h_ref, ya_ref, yb_ref, yc_ref, yd_ref, wo_ref, g_ref, wg_ref, wv_ref,
                 cg_ref, cv_ref, wd_ref, o_ref, buf_g, buf_v, un_s, act_s, *, tf):
    tm = h_ref.shape[1]
    gw = GROUP_WIDTH
    d_ff = wg_ref.shape[1]

    @pl.when(pl.program_id(1) == 0)
    def _():
        buf_g[0:HALO, :] = jnp.zeros((HALO, d_ff), F32)
        buf_v[0:HALO, :] = jnp.zeros((HALO, d_ff), F32)

    h1 = h_ref[0]
    for g, y_ref in enumerate((ya_ref, yb_ref, yc_ref, yd_ref)):
        h1 = h1 + _dot(y_ref[0], wo_ref[g * gw:(g + 1) * gw, :])
    ms = jnp.mean(h1 * h1, axis=-1, keepdims=True)
    un_s[...] = (h1 * lax.rsqrt(ms + EPS) * g_ref[...]).astype(BF16)
    o_ref[0] = h1

    def conv(up, buf, cw, fs):
        buf[HALO:HALO + tm, fs] = up
        out = cw[FFN_CONV - 1:FFN_CONV, fs] * up
        for j in range(FFN_CONV - 1):
            off = HALO - (FFN_CONV - 1) + j
            out = out + cw[j:j + 1, fs] * buf[off:off + tm, fs]
        buf[0:HALO, fs] = up[tm - HALO:tm, :]
        return out

    n_f = d_ff // tf
    split = (n_f // 2) * tf
    for f in range(n_f):
        fs = slice(f * tf, (f + 1) * tf)
        cg = conv(_dot(un_s[...], wg_ref[:, fs]), buf_g, cg_ref, fs)
        cv = conv(_dot(un_s[...], wv_ref[:, fs]), buf_v, cv_ref, fs)
        act_s[:, fs] = (_silu(cg) * cv).astype(BF16)
        if (f + 1) * tf == split:
            o_ref[0] += _dot(act_s[:, 0:split], wd_ref[0:split, :])
    o_ref[0] += _dot(act_s[:, split:d_ff], wd_ref[split:d_ff, :])


def _post(h, ya, yb, yc, yd, wo, g, wg, wv, cg, cv, wd, tm=512, tf=256):
    b, s, d = h.shape
    gw = GROUP_WIDTH
    d_ff = wg.shape[1]
    const = lambda shape: pl.BlockSpec(shape, lambda i, j: (0,) * len(shape))
    y_spec = pl.BlockSpec((1, tm, gw), lambda i, j: (i, j, 0))
    return pl.pallas_call(
        partial(_post_kernel, tf=tf),
        grid=(b, s // tm),
        in_specs=[
            pl.BlockSpec((1, tm, d), lambda i, j: (i, j, 0)),
            y_spec, y_spec, y_spec, y_spec,
            const((N_GROUPS * gw, d)),
            const((1, d)),
            const((d, d_ff)),
            const((d, d_ff)),
            const((FFN_CONV, d_ff)),
            const((FFN_CONV, d_ff)),
            const((d_ff, d)),
        ],
        out_specs=pl.BlockSpec((1, tm, d), lambda i, j: (i, j, 0)),
        out_shape=jax.ShapeDtypeStruct((b, s, d), F32),
        scratch_shapes=[
            pltpu.VMEM((tm + HALO, d_ff), F32),
            pltpu.VMEM((tm + HALO, d_ff), F32),
            pltpu.VMEM((tm, d), BF16),
            pltpu.VMEM((tm, d_ff), BF16),
        ],
        compiler_params=pltpu.CompilerParams(
            dimension_semantics=("parallel", "arbitrary"), vmem_limit_bytes=VMEM_LIMIT),
        name="post_ffn",
    )(h, ya, yb, yc, yd, wo, g, wg, wv, cg, cv, wd)


def _retention_tables(seq, t):
    hd = HEAD_DIM
    inv_freq = ROPE_BASE ** (-jnp.arange(0, hd, 2, dtype=F32) / hd)
    ang = jnp.arange(seq, dtype=F32)[:, None] * inv_freq[None, :]
    cos, sin = jnp.cos(ang), jnp.sin(ang)
    cos_t = jnp.tile(jnp.concatenate([cos, cos], axis=-1), (1, N_HEADS))
    sin_t = jnp.tile(jnp.concatenate([-sin, sin], axis=-1), (1, N_HEADS))
    lgh = jnp.log1p(-jnp.exp2(-RET_DECAY_EXP - jnp.arange(N_HEADS, dtype=F32)))
    n = jnp.arange(t, dtype=F32)
    diff = n[:, None] - n[None, :]
    keep = diff >= 0
    dmat = jnp.where(keep[None], jnp.exp(jnp.where(keep, diff, 0.0)[None] * lgh[:, None, None]), 0.0)
    wst = jnp.repeat(jnp.exp((n + 1.0)[:, None] * lgh[None, :]), hd, axis=1)
    wend = jnp.repeat(jnp.exp((t - 1.0 - n)[:, None] * lgh[None, :]), hd, axis=1)
    gdec = jnp.exp(t * lgh)
    return cos_t, sin_t, dmat, wst, wend, gdec


def kernel(x, norm_mix, norm_ffn, w_in, conv_qkv_a, a_log_a, dt_bias_a, onorm_a, onorm_b, qnorm_c, knorm_c,
           fbias_c, lower_bound_d, onorm_d, w_out, w_up, conv_ffn, w_down):
    b, s, d = x.shape
    depth = w_in.shape[0]
    gw = GROUP_WIDTH
    nh = N_HEADS
    d_ff = w_down.shape[1]
    t_ret = 512
    t_fox = 512

    oa = 4 * gw
    ob = oa + 2 * nh
    oc = ob + 4 * gw
    od = oc + 4 * gw + nh
    w_main = jnp.concatenate(
        [w_in[:, :, 0:oa], w_in[:, :, ob:ob + 4 * gw], w_in[:, :, oc:oc + 4 * gw], w_in[:, :, od:od + 4 * gw]],
        axis=-1).astype(BF16)
    w_small = jnp.concatenate([w_in[:, :, oa:ob], w_in[:, :, oc + 4 * gw:od]], axis=-1)
    w_small = jnp.pad(w_small, ((0, 0), (0, 0), (0, SMALL_W - 3 * nh))).astype(BF16)
    w_out_b = w_out.astype(BF16)
    w_g = w_up[:, :, :d_ff].astype(BF16)
    w_v = w_up[:, :, d_ff:].astype(BF16)
    w_down_b = w_down.astype(BF16)

    def small_row(vals, off):
        return jnp.zeros((depth, 1, SMALL_W), F32).at[:, 0, off:off + nh].set(vals.astype(F32))

    alog_rows = small_row(a_log_a, nh)
    dtb_rows = small_row(dt_bias_a, nh)
    fb_rows = small_row(fbias_c, 2 * nh)

    lbs = jax.nn.softmax(lower_bound_d.astype(F32), axis=0)
    lbs = jnp.cumsum(lbs, axis=0) - lbs[0]

    ltri = jnp.asarray(np.tril(np.ones((CHUNK, CHUNK), np.float32))).astype(BF16)
    utri = jnp.asarray(np.triu(np.ones((CHUNK, CHUNK), np.float32))).astype(BF16)
    utri_fox = jnp.asarray(np.triu(np.ones((t_fox, t_fox), np.float32)))
    mall_np, masks_np = _hgrn_tables()
    mall = jnp.asarray(mall_np).astype(BF16)
    masks = jnp.asarray(np.tile(masks_np, (1, 1, nh)))
    cos_t, sin_t, dmat, wst, wend, gdec = _retention_tables(s, t_ret)

    h = x.astype(F32)
    for l in range(depth):
        pm, ps = _inproj(h.reshape(b * s, d), norm_mix[l].reshape(1, d).astype(F32), w_main[l], w_small[l])
        pm = pm.reshape(b, s, 4 * 4 * gw)
        ps = ps.reshape(b, s, SMALL_W)
        ya = _gdn(pm, ps, conv_qkv_a[l].astype(F32), alog_rows[l], dtb_rows[l],
                  jnp.tile(onorm_a[l].reshape(1, HEAD_DIM).astype(F32), (1, nh)), ltri, utri)
        yb = _ret(pm, cos_t, sin_t, dmat, wst, wend, gdec, onorm_b[l].reshape(1, HEAD_DIM).astype(F32), t_ret)
        qh, kh, vh, c = _fox_prep(pm, ps, jnp.tile(qnorm_c[l].reshape(1, HEAD_DIM).astype(F32), (1, nh)),
                                  jnp.tile(knorm_c[l].reshape(1, HEAD_DIM).astype(F32), (1, nh)),
                                  fb_rows[l], utri_fox, t_fox)
        yc = _fox(qh, kh, vh, c, pm, t_fox)
        yd = _hgrn(pm, lbs[l].reshape(1, gw), jnp.tile(onorm_d[l].reshape(1, HEAD_DIM).astype(F32), (1, nh)),
                   mall, masks)
        h = _post(h, ya, yb, yc, yd, w_out_b[l], norm_ffn[l].reshape(1, d).astype(F32), w_g[l], w_v[l],
                  conv_ffn[l][:, :d_ff].astype(F32), conv_ffn[l][:, d_ff:].astype(F32), w_down_b[l])
    return h.astype(x.dtype)
```

```python
from functools import partial

import numpy as np
import jax
import jax.numpy as jnp
from jax import lax
from jax.experimental import pallas as pl
from jax.experimental.pallas import tpu as pltpu

F32 = jnp.float32
BF16 = jnp.bfloat16
HI = lax.Precision.HIGHEST

N_GROUPS = 4
HEAD_DIM = 64
N_HEADS = 4
GROUP_WIDTH = N_HEADS * HEAD_DIM
SHORT_CONV = 4
FFN_CONV = 3
ROPE_BASE = 10000.0
RET_DECAY_EXP = 5.0
EPS = 1e-6
NEG_BIG = -1e30
CHUNK = 64
SMALL_W = 128
HALO = 8
VMEM_LIMIT = 56 * 1024 * 1024


def _dot(a, b, precision=None):
    return jnp.dot(a, b, preferred_element_type=F32, precision=precision)


def _dot_nt(a, b, precision=None):
    return lax.dot_general(a, b, (((1,), (1,)), ((), ())), preferred_element_type=F32, precision=precision)


def _dot_tn(a, b, precision=None):
    return lax.dot_general(a, b, (((0,), (0,)), ((), ())), preferred_element_type=F32, precision=precision)


def _sigmoid(x):
    return 1.0 / (1.0 + jnp.exp(-x))


def _silu(x):
    return x * _sigmoid(x)


def _softplus(x):
    return jnp.maximum(x, 0.0) + jnp.log1p(jnp.exp(-jnp.abs(x)))


def _head_slice(h):
    return slice(h * HEAD_DIM, (h + 1) * HEAD_DIM)


def _inproj_kernel(x_ref, g_ref, wm_ref, ws_ref, pm_ref, ps_ref, un_ref, *, tn):
    x = x_ref[...]
    ms = jnp.mean(x * x, axis=-1, keepdims=True)
    un_ref[...] = (x * lax.rsqrt(ms + EPS) * g_ref[...]).astype(BF16)
    ps_ref[...] = _dot(un_ref[...], ws_ref[...])
    for j in range(wm_ref.shape[1] // tn):
        ns = slice(j * tn, (j + 1) * tn)
        pm_ref[:, ns] = _dot(un_ref[...], wm_ref[:, ns]).astype(BF16)


def _inproj(h2d, g, wm, ws, tm=1024, tn=1024):
    m, d = h2d.shape
    n = wm.shape[1]
    return pl.pallas_call(
        partial(_inproj_kernel, tn=tn),
        grid=(m // tm,),
        in_specs=[
            pl.BlockSpec((tm, d), lambda i: (i, 0)),
            pl.BlockSpec((1, d), lambda i: (0, 0)),
            pl.BlockSpec((d, n), lambda i: (0, 0)),
            pl.BlockSpec((d, SMALL_W), lambda i: (0, 0)),
        ],
        out_specs=[
            pl.BlockSpec((tm, n), lambda i: (i, 0)),
            pl.BlockSpec((tm, SMALL_W), lambda i: (i, 0)),
        ],
        out_shape=[
            jax.ShapeDtypeStruct((m, n), BF16),
            jax.ShapeDtypeStruct((m, SMALL_W), F32),
        ],
        scratch_shapes=[pltpu.VMEM((tm, d), BF16)],
        compiler_params=pltpu.CompilerParams(
            dimension_semantics=("parallel",), vmem_limit_bytes=VMEM_LIMIT),
        name="inproj",
    )(h2d, g, wm, ws)


def _dot01(m, x):
    hi = x.astype(BF16)
    lo = (x - hi.astype(F32)).astype(BF16)
    return _dot(m, hi) + _dot(m, lo)


def _bdot(a, b):
    return lax.dot_general(a, b, (((2,), (1,)), ((0,), (0,))), preferred_element_type=F32)


def _bdot_nt(a, b):
    return lax.dot_general(a, b, (((2,), (2,)), ((0,), (0,))), preferred_element_type=F32)


def _bdot_tn(a, b):
    return lax.dot_general(a, b, (((1,), (1,)), ((0,), (0,))), preferred_element_type=F32)


GDN_GROUP = 8


def _dot01_right(x, m):
    hi = x.astype(BF16)
    lo = (x - hi.astype(F32)).astype(BF16)
    return _dot(hi, m) + _dot(lo, m)


def _gdn_kernel(p_ref, ps_ref, cw_ref, alog_ref, dtb_ref, on_ref, ltri_ref, utri_ref,
                o_ref, xbuf, q_s, k_s, v_s, la_s, be_s, g_s, qe_s, o0_s, p_s, n_s, state):
    t = p_ref.shape[1]
    gw = GROUP_WIDTH

    @pl.when(pl.program_id(1) == 0)
    def _():
        xbuf[0:HALO, :] = jnp.zeros((HALO, 3 * gw), F32)
        state[...] = jnp.zeros_like(state)
        p_s[...] = jnp.zeros_like(p_s)
        n_s[...] = jnp.zeros_like(n_s)

    x = p_ref[0, :, 0:3 * gw].astype(F32)
    xbuf[HALO:HALO + t, :] = x
    cw = cw_ref[...]
    y = cw[3:4, :] * x
    for j in range(SHORT_CONV - 1):
        off = HALO - (SHORT_CONV - 1) + j
        y = y + cw[j:j + 1, :] * xbuf[off:off + t, :]
    xbuf[0:HALO, :] = x[t - HALO:t, :]
    y = _silu(y)
    brow = lax.broadcasted_iota(jnp.int32, (gw, gw), 0) // HEAD_DIM
    bcol = lax.broadcasted_iota(jnp.int32, (gw, gw), 1) // HEAD_DIM
    same_head = brow == bcol
    head_ones = same_head.astype(BF16)
    q = y[:, 0:gw]
    k = y[:, gw:2 * gw]
    q_s[...] = q * lax.rsqrt(_dot01_right(q * q, head_ones) + EPS) * HEAD_DIM ** -0.5
    k_s[...] = k * lax.rsqrt(_dot01_right(k * k, head_ones) + EPS)
    v_s[...] = y[:, 2 * gw:3 * gw]

    small = ps_ref[0]
    la_s[...] = -jnp.exp(alog_ref[...]) * _softplus(small + dtb_ref[...])
    be_s[...] = _sigmoid(small)

    row = lax.broadcasted_iota(jnp.int32, (CHUNK, CHUNK), 0)
    col = lax.broadcasted_iota(jnp.int32, (CHUNK, CHUNK), 1)
    tril = row >= col
    strict = row > col
    eye = (row == col).astype(F32)
    ones8 = jnp.ones((8, CHUNK), BF16)
    ltri = ltri_ref[...]
    utri = utri_ref[...].astype(F32)

    def factor_body(ci, carry):
        rows, qs, ks, vs, betas, gcs, grs = [], [], [], [], [], [], []
        for g in range(GDN_GROUP):
            r = pl.ds(pl.multiple_of((ci * GDN_GROUP + g) * CHUNK, CHUNK), CHUNK)
            rows.append(r)
            la_c = la_s[r, :]
            gcol_all = _dot01(ltri, la_c)
            g_s[r, :] = gcol_all
            be_c = be_s[r, :]
            for h in range(N_HEADS):
                hs = _head_slice(h)
                qs.append(q_s[r, hs])
                ks.append(k_s[r, hs])
                vs.append(v_s[r, hs])
                betas.append(be_c[:, h:h + 1])
                gcs.append(gcol_all[:, N_HEADS + h:N_HEADS + h + 1])
                la_col = la_c[:, N_HEADS + h:N_HEADS + h + 1]
                grs.append(_dot01(ones8, la_col * utri)[0:1, :])
        q = jnp.stack(qs)
        k = jnp.stack(ks)
        v = jnp.stack(vs)
        beta = jnp.stack(betas)
        gc = jnp.stack(gcs)
        gr = jnp.stack(grs)
        gam = jnp.where(tril, jnp.exp(jnp.where(tril, gc - gr, 0.0)), 0.0)
        kb = k * beta
        kbf = k.astype(BF16)
        a_mat = jnp.where(strict, _bdot_nt(kb.astype(BF16), kbf) * gam, 0.0)
        tinv = eye - a_mat
        pw = a_mat
        for _ in range(5):
            pwb = pw.astype(BF16)
            pw = _bdot(pwb, pwb)
            tinv = tinv + _bdot(tinv.astype(BF16), pw.astype(BF16))
        tb = tinv.astype(BF16)
        eg = jnp.exp(gc)
        gend = gc[:, CHUNK - 1:CHUNK, :]
        u = _bdot(tb, (v * beta).astype(BF16))
        w = _bdot(tb, (kb * eg).astype(BF16)).astype(BF16)
        qk = (_bdot_nt(q.astype(BF16), kbf) * gam).astype(BF16)
        qg = q * eg
        kd = (k * jnp.exp(gend - gc)).astype(BF16)
        ub = u.astype(BF16)
        pmat = _bdot_tn(kd, w).astype(BF16)
        nmat = _bdot_tn(kd, ub)
        qe = (qg - _bdot(qk, w)).astype(BF16)
        o0 = _bdot(qk, ub)
        for g in range(GDN_GROUP):
            cidx = ci * GDN_GROUP + g
            for h in range(N_HEADS):
                i = g * N_HEADS + h
                hs = _head_slice(h)
                qe_s[rows[g], hs] = qe[i]
                o0_s[rows[g], hs] = o0[i]
                p_s[cidx, hs, hs] = pmat[i]
                n_s[cidx, hs, hs] = nmat[i]
        return carry

    lax.fori_loop(0, t // (CHUNK * GDN_GROUP), factor_body, 0)

    lane_head = lax.broadcasted_iota(jnp.int32, (1, gw), 1) // HEAD_DIM
    onorm = on_ref[...]

    def scan_body(c, carry):
        r = pl.ds(pl.multiple_of(c * CHUNK, CHUNK), CHUNK)
        s_f = state[...]
        s_b = s_f.astype(BF16)
        o = _dot(qe_s[r, :], s_b) + o0_s[r, :]
        gend = g_s[r, :][CHUNK - 1:CHUNK, :]
        ge_row = jnp.zeros((1, gw), F32)
        for h in range(N_HEADS):
            ge_row = jnp.where(lane_head == h, jnp.exp(gend[:, N_HEADS + h:N_HEADS + h + 1]), ge_row)
        state[...] = s_f * ge_row - _dot(p_s[c], s_b) + n_s[c]
        ms = _dot01_right(o * o, head_ones) * (1.0 / HEAD_DIM)
        gate = p_ref[0, r, 3 * gw:4 * gw].astype(F32)
        o_ref[0, r, :] = (o * lax.rsqrt(ms + EPS) * onorm * _silu(gate)).astype(o_ref.dtype)
        return carry

    lax.fori_loop(0, t // CHUNK, scan_body, 0, unroll=8)


def _gdn(p, ps, cw, alog_row, dtb_row, onorm_t, ltri, utri, t=512):
    b, s, _ = p.shape
    gw = GROUP_WIDTH
    const = lambda shape: pl.BlockSpec(shape, lambda i, j: (0,) * len(shape))
    return pl.pallas_call(
        _gdn_kernel,
        grid=(b, s // t),
        in_specs=[
            pl.BlockSpec((1, t, 4 * gw), lambda i, j: (i, j, 0)),
            pl.BlockSpec((1, t, SMALL_W), lambda i, j: (i, j, 0)),
            const((SHORT_CONV, 3 * gw)),
            const((1, SMALL_W)),
            const((1, SMALL_W)),
            const((1, gw)),
            const((CHUNK, CHUNK)),
            const((CHUNK, CHUNK)),
        ],
        out_specs=pl.BlockSpec((1, t, gw), lambda i, j: (i, j, 0)),
        out_shape=jax.ShapeDtypeStruct((b, s, gw), BF16),
        scratch_shapes=[
            pltpu.VMEM((t + HALO, 3 * gw), F32),
            pltpu.VMEM((t, gw), F32),
            pltpu.VMEM((t, gw), F32),
            pltpu.VMEM((t, gw), F32),
            pltpu.VMEM((t, SMALL_W), F32),
            pltpu.VMEM((t, SMALL_W), F32),
            pltpu.VMEM((t, SMALL_W), F32),
            pltpu.VMEM((t, gw), BF16),
            pltpu.VMEM((t, gw), F32),
            pltpu.VMEM((t // CHUNK, gw, gw), BF16),
            pltpu.VMEM((t // CHUNK, gw, gw), F32),
            pltpu.VMEM((gw, gw), F32),
        ],
        compiler_params=pltpu.CompilerParams(
            dimension_semantics=("parallel", "arbitrary"), vmem_limit_bytes=VMEM_LIMIT),
        name="gdn",
    )(p, ps, cw, alog_row, dtb_row, onorm_t, ltri, utri)


def _ret_kernel(p_ref, cos_ref, sin_ref, dmat_ref, wst_ref, wend_ref, gdec_ref, on_ref,
                o_ref, state):
    gw = GROUP_WIDTH

    @pl.when(pl.program_id(1) == 0)
    def _():
        state[...] = jnp.zeros_like(state)

    q = p_ref[0, :, 0:gw].astype(F32)
    k = p_ref[0, :, gw:2 * gw].astype(F32)
    lane = lax.broadcasted_iota(jnp.int32, q.shape, 1)
    first_half = (lane % HEAD_DIM) < (HEAD_DIM // 2)
    cos = cos_ref[...]
    sin = sin_ref[...]

    def rope(x):
        rot = jnp.where(first_half, pltpu.roll(x, gw - HEAD_DIM // 2, 1), pltpu.roll(x, HEAD_DIM // 2, 1))
        return x * cos + rot * sin

    q = rope(q)
    k = rope(k) * HEAD_DIM ** -0.5
    qs = (q * wst_ref[...]).astype(BF16)
    ke = (k * wend_ref[...]).astype(BF16)
    qb = q.astype(BF16)
    kb = k.astype(BF16)
    onorm = on_ref[...]
    for h in range(N_HEADS):
        hs = _head_slice(h)
        v = p_ref[0, :, 2 * gw + h * HEAD_DIM:2 * gw + (h + 1) * HEAD_DIM]
        scores = (_dot_nt(qb[:, hs], kb[:, hs]) * dmat_ref[h]).astype(BF16)
        s_h = state[h]
        o = _dot(scores, v) + _dot(qs[:, hs], s_h.astype(BF16))
        state[h] = gdec_ref[h] * s_h + _dot_tn(ke[:, hs], v)
        oc = o - jnp.mean(o, axis=-1, keepdims=True)
        o = oc * lax.rsqrt(jnp.mean(oc * oc, axis=-1, keepdims=True) + EPS) * onorm
        gate = p_ref[0, :, 3 * gw + h * HEAD_DIM:3 * gw + (h + 1) * HEAD_DIM].astype(F32)
        o_ref[0, :, hs] = (o * _silu(gate)).astype(o_ref.dtype)


def _ret(p, cos, sin, dmat, wst, wend, gdec, onorm, t):
    b, s, _ = p.shape
    gw = GROUP_WIDTH
    return pl.pallas_call(
        _ret_kernel,
        grid=(b, s // t),
        in_specs=[
            pl.BlockSpec((1, t, 4 * gw), lambda i, j: (i, j, 1)),
            pl.BlockSpec((t, gw), lambda i, j: (j, 0)),
            pl.BlockSpec((t, gw), lambda i, j: (j, 0)),
            pl.BlockSpec((N_HEADS, t, t), lambda i, j: (0, 0, 0)),
            pl.BlockSpec((t, gw), lambda i, j: (0, 0)),
            pl.BlockSpec((t, gw), lambda i, j: (0, 0)),
            pl.BlockSpec(memory_space=pltpu.SMEM),
            pl.BlockSpec((1, HEAD_DIM), lambda i, j: (0, 0)),
        ],
        out_specs=pl.BlockSpec((1, t, gw), lambda i, j: (i, j, 0)),
        out_shape=jax.ShapeDtypeStruct((b, s, gw), BF16),
        scratch_shapes=[pltpu.VMEM((N_HEADS, HEAD_DIM, HEAD_DIM), F32)],
        compiler_params=pltpu.CompilerParams(
            dimension_semantics=("parallel", "arbitrary"), vmem_limit_bytes=VMEM_LIMIT),
        name="retention",
    )(p, cos, sin, dmat, wst, wend, gdec, onorm)


def _fox_prep_kernel(p_ref, ps_ref, qn_ref, kn_ref, fb_ref, utri_ref,
                     q_out, k_out, v_out, c_out, carry):
    gw = GROUP_WIDTH

    @pl.when(pl.program_id(1) == 0)
    def _():
        carry[...] = jnp.zeros_like(carry)

    brow = lax.broadcasted_iota(jnp.int32, (gw, gw), 0) // HEAD_DIM
    bcol = lax.broadcasted_iota(jnp.int32, (gw, gw), 1) // HEAD_DIM
    head_ones = (brow == bcol).astype(BF16)
    q = p_ref[0, :, 0:gw].astype(F32)
    k = p_ref[0, :, gw:2 * gw].astype(F32)
    q = q * lax.rsqrt(_dot01_right(q * q, head_ones) * (1.0 / HEAD_DIM) + EPS) * qn_ref[...] * HEAD_DIM ** -0.5
    k = k * lax.rsqrt(_dot01_right(k * k, head_ones) * (1.0 / HEAD_DIM) + EPS) * kn_ref[...]
    qb = q.astype(BF16)
    kb = k.astype(BF16)
    ones_col = (lax.broadcasted_iota(jnp.int32, (p_ref.shape[1], HEAD_DIM), 1) == 0).astype(BF16)
    for h in range(N_HEADS):
        q_out[0, h] = qb[:, _head_slice(h)]
        k_out[0, h] = kb[:, _head_slice(h)]
        v_out[0, h, :, 0:HEAD_DIM] = p_ref[0, :, 2 * gw + h * HEAD_DIM:2 * gw + (h + 1) * HEAD_DIM]
        v_out[0, h, :, HEAD_DIM:2 * HEAD_DIM] = ones_col

    x = ps_ref[0] + fb_ref[...]
    logf = jnp.minimum(x, 0.0) - jnp.log1p(jnp.exp(-jnp.abs(x)))
    logf_t = logf.T[8:16, :]
    c = _dot(logf_t, utri_ref[...], HI) + carry[:, 0:1]
    c_out[0, 0] = c
    t = c.shape[1]
    carry[...] = jnp.broadcast_to(c[:, t - 1:t], carry.shape)


def _fox_prep(p, ps, qn, kn, fb_row, utri, t):
    b, s, _ = p.shape
    gw = GROUP_WIDTH
    hm = jax.ShapeDtypeStruct((b, N_HEADS, s, HEAD_DIM), BF16)
    hm_spec = pl.BlockSpec((1, N_HEADS, t, HEAD_DIM), lambda i, j: (i, 0, j, 0))
    hv = jax.ShapeDtypeStruct((b, N_HEADS, s, 2 * HEAD_DIM), BF16)
    hv_spec = pl.BlockSpec((1, N_HEADS, t, 2 * HEAD_DIM), lambda i, j: (i, 0, j, 0))
    return pl.pallas_call(
        _fox_prep_kernel,
        grid=(b, s // t),
        in_specs=[
            pl.BlockSpec((1, t, 4 * gw), lambda i, j: (i, j, 2)),
            pl.BlockSpec((1, t, SMALL_W), lambda i, j: (i, j, 0)),
            pl.BlockSpec((1, gw), lambda i, j: (0, 0)),
            pl.BlockSpec((1, gw), lambda i, j: (0, 0)),
            pl.BlockSpec((1, SMALL_W), lambda i, j: (0, 0)),
            pl.BlockSpec((t, t), lambda i, j: (0, 0)),
        ],
        out_specs=[hm_spec, hm_spec, hv_spec, pl.BlockSpec((1, 1, 8, t), lambda i, j: (i, j, 0, 0))],
        out_shape=[hm, hm, hv, jax.ShapeDtypeStruct((b, s // t, 8, t), F32)],
        scratch_shapes=[pltpu.VMEM((8, 128), F32)],
        compiler_params=pltpu.CompilerParams(
            dimension_semantics=("parallel", "arbitrary"), vmem_limit_bytes=VMEM_LIMIT),
        name="fox_prep",
    )(p, ps, qn, kn, fb_row, utri)


FOX_STRIP = 32
LANES = 128


def _fox_kernel(q_ref, k_ref, v_ref, c_ref, g_ref, o_ref, m_s, acc_s, s_scr, p_scr, al_scr):
    qi = pl.program_id(1)
    tq = q_ref.shape[2]
    tk = c_ref.shape[3]
    nj = tk // LANES

    m_s[...] = jnp.full_like(m_s, NEG_BIG)
    acc_s[...] = jnp.zeros_like(acc_s)

    def update(ki, masked, slot):
        kr = pl.ds(pl.multiple_of(ki * tk, tk), tk)
        for h in range(N_HEADS):
            s_scr[slot, h] = _dot_nt(q_ref[0, h], k_ref[0, h, kr, :])
        row = lax.broadcasted_iota(jnp.int32, (FOX_STRIP, LANES), 0)
        col = lax.broadcasted_iota(jnp.int32, (FOX_STRIP, LANES), 1)
        for h in range(N_HEADS):
            c_blk = [c_ref[0, ki, h:h + 1, j * LANES:(j + 1) * LANES] for j in range(nj)]
            for i in range(tq // FOX_STRIP):
                r0 = i * FOX_STRIP
                r = slice(r0, r0 + FOX_STRIP)
                live = min(nj, (r0 + FOX_STRIP - 1) // LANES + 1) if masked else nj
                sb = [s_scr[slot, h, r, j * LANES:(j + 1) * LANES] - c_blk[j] for j in range(live)]
                if masked:
                    sb = [jnp.where(row + r0 >= col + j * LANES, sb[j], NEG_BIG)
                          if (j + 1) * LANES - 1 > r0 else sb[j] for j in range(live)]
                mx = sb[0]
                for j in range(1, live):
                    mx = jnp.maximum(mx, sb[j])
                m_old = m_s[h, r, :]
                m_new = jnp.maximum(m_old, jnp.max(mx, axis=-1, keepdims=True))
                m_s[h, r, :] = m_new
                al_scr[slot, h, r, :] = jnp.exp(m_old - m_new)
                for j in range(live):
                    p_scr[slot, h, r, j * LANES:(j + 1) * LANES] = jnp.exp(sb[j] - m_new).astype(BF16)
                for j in range(live, nj):
                    p_scr[slot, h, r, j * LANES:(j + 1) * LANES] = jnp.zeros((FOX_STRIP, LANES), BF16)
        for h in range(N_HEADS):
            acc_s[h] = al_scr[slot, h] * acc_s[h] + _dot(p_scr[slot, h], v_ref[0, h, kr, :])

    def below_diagonal_pair(pi, carry):
        update(2 * pi, False, 0)
        update(2 * pi + 1, False, 1)
        return carry

    lax.fori_loop(0, qi // 2, below_diagonal_pair, 0)

    @pl.when(qi % 2 == 1)
    def _():
        update(qi - 1, False, 0)

    update(qi, True, 1)
    for h in range(N_HEADS):
        hs = _head_slice(h)
        acc = acc_s[h]
        o = acc[:, 0:HEAD_DIM] / acc[:, HEAD_DIM:HEAD_DIM + 1]
        gate = g_ref[0, :, hs].astype(F32)
        o_ref[0, :, hs] = (o * _sigmoid(gate)).astype(o_ref.dtype)


def _fox(qh, kh, vh, c, p, t):
    b, _, s, _ = qh.shape
    gw = GROUP_WIDTH
    n = s // t
    return pl.pallas_call(
        _fox_kernel,
        grid=(b, n),
        in_specs=[
            pl.BlockSpec((1, N_HEADS, t, HEAD_DIM), lambda i, j: (i, 0, j, 0)),
            pl.BlockSpec((1, N_HEADS, s, HEAD_DIM), lambda i, j: (i, 0, 0, 0), pipeline_mode=pl.Buffered(1)),
            pl.BlockSpec((1, N_HEADS, s, 2 * HEAD_DIM), lambda i, j: (i, 0, 0, 0), pipeline_mode=pl.Buffered(1)),
            pl.BlockSpec((1, n, 8, t), lambda i, j: (i, 0, 0, 0), pipeline_mode=pl.Buffered(1)),
            pl.BlockSpec((1, t, gw), lambda i, j: (i, j, 4 * 2 + 3)),
        ],
        out_specs=pl.BlockSpec((1, t, gw), lambda i, j: (i, j, 0)),
        out_shape=jax.ShapeDtypeStruct((b, s, gw), BF16),
        scratch_shapes=[
            pltpu.VMEM((N_HEADS, t, LANES), F32),
            pltpu.VMEM((N_HEADS, t, 2 * HEAD_DIM), F32),
            pltpu.VMEM((2, N_HEADS, t, t), F32),
            pltpu.VMEM((2, N_HEADS, t, t), BF16),
            pltpu.VMEM((2, N_HEADS, t, LANES), F32),
        ],
        compiler_params=pltpu.CompilerParams(
            dimension_semantics=("parallel", "parallel"), vmem_limit_bytes=VMEM_LIMIT),
        name="fox_attn",
    )(qh, kh, vh, c, p)


N_LEVELS = 6


def _hgrn_tables():
    c = CHUNK
    mall = np.zeros((N_LEVELS + 2, c, c), np.float32)
    masks = np.zeros((N_LEVELS + 1, c, c), np.float32)
    for lv in range(N_LEVELS):
        half = 1 << lv
        for t in range(c):
            m = (t >> (lv + 1)) * (2 * half) + half
            if t >= m:
                mall[lv, t, m:t + 1] = 1.0
            else:
                mall[lv, t, t + 1:m] = 1.0
            for s in range(c):
                if (s >> (lv + 1)) == (t >> (lv + 1)) and t >= m and s < m:
                    masks[lv, t, s] = 1.0
    masks[N_LEVELS] = np.eye(c, dtype=np.float32)
    for t in range(c):
        mall[N_LEVELS, t, :t + 1] = 1.0
        mall[N_LEVELS + 1, t, t + 1:] = 1.0
    return mall.reshape((N_LEVELS + 2) * c, c), masks


def _hgrn_kernel(p_ref, lb_ref, on_ref, mall_ref, mask_ref, o_ref, state):
    t = p_ref.shape[1]
    gw = GROUP_WIDTH

    @pl.when(pl.program_id(1) == 0)
    def _():
        state[...] = jnp.zeros_like(state)

    lb = lb_ref[...]
    onorm = on_ref[...]
    mall = mall_ref[...]
    nl = N_LEVELS + 1
    brow = lax.broadcasted_iota(jnp.int32, (gw, gw), 0) // HEAD_DIM
    bcol = lax.broadcasted_iota(jnp.int32, (gw, gw), 1) // HEAD_DIM
    same_head = brow == bcol
    head_ones = same_head.astype(BF16)
    zero_b = jnp.zeros((), BF16)

    def per_head_rows(x):
        return jnp.where(same_head, jnp.concatenate([x] * N_HEADS, axis=0), zero_b)

    def chunk_body(c, carry):
        r = pl.ds(pl.multiple_of(c * CHUNK, CHUNK), CHUNK)
        qx = p_ref[0, r, 0:gw].astype(F32)
        f = p_ref[0, r, gw:2 * gw].astype(F32)
        logf = jnp.log(lb + (1.0 - lb) * _sigmoid(f))
        kk = (1.0 - lb) * _sigmoid(-f)
        qq = _silu(qx)
        x_all = jnp.exp(_dot(mall, logf.astype(BF16)))
        x_q = x_all[N_LEVELS * CHUNK:(N_LEVELS + 1) * CHUNK, :]
        x_k = x_all[(N_LEVELS + 1) * CHUNK:(N_LEVELS + 2) * CHUNK, :]
        scores = None
        for lv in range(nl):
            if lv < N_LEVELS:
                x_l = x_all[lv * CHUNK:(lv + 1) * CHUNK, :]
                q_l = (qq * x_l).astype(BF16)
                k_l = (kk * x_l).astype(BF16)
            else:
                q_l = qq.astype(BF16)
                k_l = kk.astype(BF16)
            term = _dot_nt(q_l, per_head_rows(k_l)) * mask_ref[lv]
            scores = term if scores is None else scores + term
        v = p_ref[0, r, 2 * gw:3 * gw]
        st = state[...]
        o = _dot(scores.astype(BF16), per_head_rows(v)) + _dot_nt((qq * x_q).astype(BF16), st.astype(BF16))
        state[...] = st * x_q[CHUNK - 1:CHUNK, :] + jnp.where(
            same_head, _dot_tn(v, (kk * x_k).astype(BF16)), 0.0)
        ms = _dot01_right(o * o, head_ones) * (1.0 / HEAD_DIM)
        gate = p_ref[0, r, 3 * gw:4 * gw].astype(F32)
        o_ref[0, r, :] = (o * lax.rsqrt(ms + EPS) * onorm * _silu(gate)).astype(o_ref.dtype)
        return carry

    lax.fori_loop(0, t // CHUNK, chunk_body, 0, unroll=8)


def _hgrn(p, lb_row, onorm, mall, masks, t=512):
    b, s, _ = p.shape
    gw = GROUP_WIDTH
    return pl.pallas_call(
        _hgrn_kernel,
        grid=(b, s // t),
        in_specs=[
            pl.BlockSpec((1, t, 4 * gw), lambda i, j: (i, j, 3)),
            pl.BlockSpec((1, gw), lambda i, j: (0, 0)),
            pl.BlockSpec((1, gw), lambda i, j: (0, 0)),
            pl.BlockSpec(mall.shape, lambda i, j: (0, 0)),
            pl.BlockSpec(masks.shape, lambda i, j: (0, 0, 0)),
        ],
        out_specs=pl.BlockSpec((1, t, gw), lambda i, j: (i, j, 0)),
        out_shape=jax.ShapeDtypeStruct((b, s, gw), BF16),
        scratch_shapes=[pltpu.VMEM((gw, gw), F32)],
        compiler_params=pltpu.CompilerParams(
            dimension_semantics=("parallel", "arbitrary"), vmem_limit_bytes=VMEM_LIMIT),
        name="hgrn2",
    )(p, lb_row, onorm, mall, masks)


def _post_kernel(h_ref, ya_ref, yb_ref, yc_ref, yd_ref, wo_ref, g_ref, wg_ref, wv_ref,
                 cg_ref, cv_ref, wd_ref, o_ref, buf_g, buf_v, un_s, act_s, *, tf):
    tm = h_ref.shape[1]
    gw = GROUP_WIDTH
    d_ff = wg_ref.shape[1]

    @pl.when(pl.program_id(1) == 0)
    def _():
        buf_g[0:HALO, :] = jnp.zeros((HALO, d_ff), F32)
        buf_v[0:HALO, :] = jnp.zeros((HALO, d_ff), F32)

    h1 = h_ref[0]
    for g, y_ref in enumerate((ya_ref, yb_ref, yc_ref, yd_ref)):
        h1 = h1 + _dot(y_ref[0], wo_ref[g * gw:(g + 1) * gw, :])
    ms = jnp.mean(h1 * h1, axis=-1, keepdims=True)
    un_s[...] = (h1 * lax.rsqrt(ms + EPS) * g_ref[...]).astype(BF16)
    o_ref[0] = h1

    def conv(up, buf, cw, fs):
        buf[HALO:HALO + tm, fs] = up
        out = cw[FFN_CONV - 1:FFN_CONV, fs] * up
        for j in range(FFN_CONV - 1):
            off = HALO - (FFN_CONV - 1) + j
            out = out + cw[j:j + 1, fs] * buf[off:off + tm, fs]
        buf[0:HALO, fs] = up[tm - HALO:tm, :]
        return out

    n_f = d_ff // tf
    split = (n_f // 2) * tf
    for f in range(n_f):
        fs = slice(f * tf, (f + 1) * tf)
        cg = conv(_dot(un_s[...], wg_ref[:, fs]), buf_g, cg_ref, fs)
        cv = conv(_dot(un_s[...], wv_ref[:, fs]), buf_v, cv_ref, fs)
        act_s[:, fs] = (_silu(cg) * cv).astype(BF16)
        if (f + 1) * tf == split:
            o_ref[0] += _dot(act_s[:, 0:split], wd_ref[0:split, :])
    o_ref[0] += _dot(act_s[:, split:d_ff], wd_ref[split:d_ff, :])


def _post(h, ya, yb, yc, yd, wo, g, wg, wv, cg, cv, wd, tm=512, tf=256):
    b, s, d = h.shape
    gw = GROUP_WIDTH
    d_ff = wg.shape[1]
    const = lambda shape: pl.BlockSpec(shape, lambda i, j: (0,) * len(shape))
    y_spec = pl.BlockSpec((1, tm, gw), lambda i, j: (i, j, 0))
    return pl.pallas_call(
        partial(_post_kernel, tf=tf),
        grid=(b, s // tm),
        in_specs=[
            pl.BlockSpec((1, tm, d), lambda i, j: (i, j, 0)),
            y_spec, y_spec, y_spec, y_spec,
            const((N_GROUPS * gw, d)),
            const((1, d)),
            const((d, d_ff)),
            const((d, d_ff)),
            const((FFN_CONV, d_ff)),
            const((FFN_CONV, d_ff)),
            const((d_ff, d)),
        ],
        out_specs=pl.BlockSpec((1, tm, d), lambda i, j: (i, j, 0)),
        out_shape=jax.ShapeDtypeStruct((b, s, d), F32),
        scratch_shapes=[
            pltpu.VMEM((tm + HALO, d_ff), F32),
            pltpu.VMEM((tm + HALO, d_ff), F32),
            pltpu.VMEM((tm, d), BF16),
            pltpu.VMEM((tm, d_ff), BF16),
        ],
        compiler_params=pltpu.CompilerParams(
            dimension_semantics=("parallel", "arbitrary"), vmem_limit_bytes=VMEM_LIMIT),
        name="post_ffn",
    )(h, ya, yb, yc, yd, wo, g, wg, wv, cg, cv, wd)


def _retention_tables(seq, t):
    hd = HEAD_DIM
    inv_freq = ROPE_BASE ** (-jnp.arange(0, hd, 2, dtype=F32) / hd)
    ang = jnp.arange(seq, dtype=F32)[:, None] * inv_freq[None, :]
    cos, sin = jnp.cos(ang), jnp.sin(ang)
    cos_t = jnp.tile(jnp.concatenate([cos, cos], axis=-1), (1, N_HEADS))
    sin_t = jnp.tile(jnp.concatenate([-sin, sin], axis=-1), (1, N_HEADS))
    lgh = jnp.log1p(-jnp.exp2(-RET_DECAY_EXP - jnp.arange(N_HEADS, dtype=F32)))
    n = jnp.arange(t, dtype=F32)
    diff = n[:, None] - n[None, :]
    keep = diff >= 0
    dmat = jnp.where(keep[None], jnp.exp(jnp.where(keep, diff, 0.0)[None] * lgh[:, None, None]), 0.0)
    wst = jnp.repeat(jnp.exp((n + 1.0)[:, None] * lgh[None, :]), hd, axis=1)
    wend = jnp.repeat(jnp.exp((t - 1.0 - n)[:, None] * lgh[None, :]), hd, axis=1)
    gdec = jnp.exp(t * lgh)
    return cos_t, sin_t, dmat, wst, wend, gdec


def kernel(x, norm_mix, norm_ffn, w_in, conv_qkv_a, a_log_a, dt_bias_a, onorm_a, onorm_b, qnorm_c, knorm_c,
           fbias_c, lower_bound_d, onorm_d, w_out, w_up, conv_ffn, w_down):
    b, s, d = x.shape
    depth = w_in.shape[0]
    gw = GROUP_WIDTH
    nh = N_HEADS
    d_ff = w_down.shape[1]
    t_ret = 512
    t_fox = 512

    oa = 4 * gw
    ob = oa + 2 * nh
    oc = ob + 4 * gw
    od = oc + 4 * gw + nh
    w_main = jnp.concatenate(
        [w_in[:, :, 0:oa], w_in[:, :, ob:ob + 4 * gw], w_in[:, :, oc:oc + 4 * gw], w_in[:, :, od:od + 4 * gw]],
        axis=-1).astype(BF16)
    w_small = jnp.concatenate([w_in[:, :, oa:ob], w_in[:, :, oc + 4 * gw:od]], axis=-1)
    w_small = jnp.pad(w_small, ((0, 0), (0, 0), (0, SMALL_W - 3 * nh))).astype(BF16)
    w_out_b = w_out.astype(BF16)
    w_g = w_up[:, :, :d_ff].astype(BF16)
    w_v = w_up[:, :, d_ff:].astype(BF16)
    w_down_b = w_down.astype(BF16)

    def small_row(vals, off):
        return jnp.zeros((depth, 1, SMALL_W), F32).at[:, 0, off:off + nh].set(vals.astype(F32))

    alog_rows = small_row(a_log_a, nh)
    dtb_rows = small_row(dt_bias_a, nh)
    fb_rows = small_row(fbias_c, 2 * nh)

    lbs = jax.nn.softmax(lower_bound_d.astype(F32), axis=0)
    lbs = jnp.cumsum(lbs, axis=0) - lbs[0]

    ltri = jnp.asarray(np.tril(np.ones((CHUNK, CHUNK), np.float32))).astype(BF16)
    utri = jnp.asarray(np.triu(np.ones((CHUNK, CHUNK), np.float32))).astype(BF16)
    utri_fox = jnp.asarray(np.triu(np.ones((t_fox, t_fox), np.float32)))
    mall_np, masks_np = _hgrn_tables()
    mall = jnp.asarray(mall_np).astype(BF16)
    masks = jnp.asarray(np.tile(masks_np, (1, 1, nh)))
    cos_t, sin_t, dmat, wst, wend, gdec = _retention_tables(s, t_ret)

    h = x.astype(F32)
    for l in range(depth):
        pm, ps = _inproj(h.reshape(b * s, d), norm_mix[l].reshape(1, d).astype(F32), w_main[l], w_small[l])
        pm = pm.reshape(b, s, 4 * 4 * gw)
        ps = ps.reshape(b, s, SMALL_W)
        ya = _gdn(pm, ps, conv_qkv_a[l].astype(F32), alog_rows[l], dtb_rows[l],
                  jnp.tile(onorm_a[l].reshape(1, HEAD_DIM).astype(F32), (1, nh)), ltri, utri)
        yb = _ret(pm, cos_t, sin_t, dmat, wst, wend, gdec, onorm_b[l].reshape(1, HEAD_DIM).astype(F32), t_ret)
        qh, kh, vh, c = _fox_prep(pm, ps, jnp.tile(qnorm_c[l].reshape(1, HEAD_DIM).astype(F32), (1, nh)),
                                  jnp.tile(knorm_c[l].reshape(1, HEAD_DIM).astype(F32), (1, nh)),
                                  fb_rows[l], utri_fox, t_fox)
        yc = _fox(qh, kh, vh, c, pm, t_fox)
        yd = _hgrn(pm, lbs[l].reshape(1, gw), jnp.tile(onorm_d[l].reshape(1, HEAD_DIM).astype(F32), (1, nh)),
                   mall, masks)
        h = _post(h, ya, yb, yc, yd, w_out_b[l], norm_ffn[l].reshape(1, d).astype(F32), w_g[l], w_v[l],
                  conv_ffn[l][:, :d_ff].astype(F32), conv_ffn[l][:, d_ff:].astype(F32), w_down_b[l])
    return h.astype(x.dtype)
```

```python
from functools import partial

import numpy as np
import jax
import jax.numpy as jnp
from jax import lax
from jax.experimental import pallas as pl
from jax.experimental.pallas import tpu as pltpu

F32 = jnp.float32
BF16 = jnp.bfloat16
HI = lax.Precision.HIGHEST

N_GROUPS = 4
HEAD_DIM = 64
N_HEADS = 4
GROUP_WIDTH = N_HEADS * HEAD_DIM
SHORT_CONV = 4
FFN_CONV = 3
ROPE_BASE = 10000.0
RET_DECAY_EXP = 5.0
EPS = 1e-6
NEG_BIG = -1e30
CHUNK = 64
SMALL_W = 128
HALO = 8
VMEM_LIMIT = 56 * 1024 * 1024


def _dot(a, b, precision=None):
    return jnp.dot(a, b, preferred_element_type=F32, precision=precision)


def _dot_nt(a, b, precision=None):
    return lax.dot_general(a, b, (((1,), (1,)), ((), ())), preferred_element_type=F32, precision=precision)


def _dot_tn(a, b, precision=None):
    return lax.dot_general(a, b, (((0,), (0,)), ((), ())), preferred_element_type=F32, precision=precision)


def _sigmoid(x):
    return 1.0 / (1.0 + jnp.exp(-x))


def _silu(x):
    return x * _sigmoid(x)


def _softplus(x):
    return jnp.maximum(x, 0.0) + jnp.log1p(jnp.exp(-jnp.abs(x)))


def _head_slice(h):
    return slice(h * HEAD_DIM, (h + 1) * HEAD_DIM)


def _inproj_kernel(x_ref, g_ref, wm_ref, ws_ref, pm_ref, ps_ref, un_ref, *, tn):
    x = x_ref[...]
    ms = jnp.mean(x * x, axis=-1, keepdims=True)
    un_ref[...] = (x * lax.rsqrt(ms + EPS) * g_ref[...]).astype(BF16)
    ps_ref[...] = _dot(un_ref[...], ws_ref[...])
    for j in range(wm_ref.shape[1] // tn):
        ns = slice(j * tn, (j + 1) * tn)
        pm_ref[:, ns] = _dot(un_ref[...], wm_ref[:, ns]).astype(BF16)


def _inproj(h2d, g, wm, ws, tm=1024, tn=1024):
    m, d = h2d.shape
    n = wm.shape[1]
    return pl.pallas_call(
        partial(_inproj_kernel, tn=tn),
        grid=(m // tm,),
        in_specs=[
            pl.BlockSpec((tm, d), lambda i: (i, 0)),
            pl.BlockSpec((1, d), lambda i: (0, 0)),
            pl.BlockSpec((d, n), lambda i: (0, 0)),
            pl.BlockSpec((d, SMALL_W), lambda i: (0, 0)),
        ],
        out_specs=[
            pl.BlockSpec((tm, n), lambda i: (i, 0)),
            pl.BlockSpec((tm, SMALL_W), lambda i: (i, 0)),
        ],
        out_shape=[
            jax.ShapeDtypeStruct((m, n), BF16),
            jax.ShapeDtypeStruct((m, SMALL_W), F32),
        ],
        scratch_shapes=[pltpu.VMEM((tm, d), BF16)],
        compiler_params=pltpu.CompilerParams(
            dimension_semantics=("parallel",), vmem_limit_bytes=VMEM_LIMIT),
        name="inproj",
    )(h2d, g, wm, ws)


def _dot01(m, x):
    hi = x.astype(BF16)
    lo = (x - hi.astype(F32)).astype(BF16)
    return _dot(m, hi) + _dot(m, lo)


def _bdot(a, b):
    return lax.dot_general(a, b, (((2,), (1,)), ((0,), (0,))), preferred_element_type=F32)


def _bdot_nt(a, b):
    return lax.dot_general(a, b, (((2,), (2,)), ((0,), (0,))), preferred_element_type=F32)


def _bdot_tn(a, b):
    return lax.dot_general(a, b, (((1,), (1,)), ((0,), (0,))), preferred_element_type=F32)


GDN_GROUP = 8


def _dot01_right(x, m):
    hi = x.astype(BF16)
    lo = (x - hi.astype(F32)).astype(BF16)
    return _dot(hi, m) + _dot(lo, m)


def _gdn_kernel(p_ref, ps_ref, cw_ref, alog_ref, dtb_ref, on_ref, ltri_ref,
                o_ref, xbuf, q_s, k_s, v_s, la_s, be_s, g_s, qe_s, o0_s, p_s, n_s, state):
    t = p_ref.shape[1]
    gw = GROUP_WIDTH

    @pl.when(pl.program_id(1) == 0)
    def _():
        xbuf[0:HALO, :] = jnp.zeros((HALO, 3 * gw), F32)
        state[...] = jnp.zeros_like(state)
        p_s[...] = jnp.zeros_like(p_s)
        n_s[...] = jnp.zeros_like(n_s)

    x = p_ref[0, :, 0:3 * gw].astype(F32)
    xbuf[HALO:HALO + t, :] = x
    cw = cw_ref[...]
    y = cw[3:4, :] * x
    for j in range(SHORT_CONV - 1):
        off = HALO - (SHORT_CONV - 1) + j
        y = y + cw[j:j + 1, :] * xbuf[off:off + t, :]
    xbuf[0:HALO, :] = x[t - HALO:t, :]
    y = _silu(y)
    brow = lax.broadcasted_iota(jnp.int32, (gw, gw), 0) // HEAD_DIM
    bcol = lax.broadcasted_iota(jnp.int32, (gw, gw), 1) // HEAD_DIM
    same_head = brow == bcol
    head_ones = same_head.astype(BF16)
    q = y[:, 0:gw]
    k = y[:, gw:2 * gw]
    q_s[...] = q * lax.rsqrt(_dot01_right(q * q, head_ones) + EPS) * HEAD_DIM ** -0.5
    k_s[...] = k * lax.rsqrt(_dot01_right(k * k, head_ones) + EPS)
    v_s[...] = y[:, 2 * gw:3 * gw]

    small = ps_ref[0]
    la_s[...] = -jnp.exp(alog_ref[...]) * _softplus(small + dtb_ref[...])
    be_s[...] = _sigmoid(small)

    row = lax.broadcasted_iota(jnp.int32, (CHUNK, CHUNK), 0)
    col = lax.broadcasted_iota(jnp.int32, (CHUNK, CHUNK), 1)
    tril = row >= col
    strict = row > col
    eye = (row == col).astype(F32)
    ltri = ltri_ref[...]

    def factor_body(ci, carry):
        rows, qs, ks, vs, betas, gcs, grs = [], [], [], [], [], [], []
        for g in range(GDN_GROUP):
            r = pl.ds(pl.multiple_of((ci * GDN_GROUP + g) * CHUNK, CHUNK), CHUNK)
            rows.append(r)
            la_c = la_s[r, :]
            gcol_all = _dot01(ltri, la_c)
            g_s[r, :] = gcol_all
            grow_all = gcol_all.T
            be_c = be_s[r, :]
            for h in range(N_HEADS):
                hs = _head_slice(h)
                qs.append(q_s[r, hs])
                ks.append(k_s[r, hs])
                vs.append(v_s[r, hs])
                betas.append(be_c[:, h:h + 1])
                gcs.append(gcol_all[:, N_HEADS + h:N_HEADS + h + 1])
                grs.append(grow_all[N_HEADS + h:N_HEADS + h + 1, :])
        q = jnp.stack(qs)
        k = jnp.stack(ks)
        v = jnp.stack(vs)
        beta = jnp.stack(betas)
        gc = jnp.stack(gcs)
        gr = jnp.stack(grs)
        gam = jnp.where(tril, jnp.exp(jnp.where(tril, gc - gr, 0.0)), 0.0)
        kb = k * beta
        kbf = k.astype(BF16)
        a_mat = jnp.where(strict, _bdot_nt(kb.astype(BF16), kbf) * gam, 0.0)
        tinv = eye - a_mat
        pw = a_mat
        for _ in range(5):
            pwb = pw.astype(BF16)
            pw = _bdot(pwb, pwb)
            tinv = tinv + _bdot(tinv.astype(BF16), pw.astype(BF16))
        tb = tinv.astype(BF16)
        eg = jnp.exp(gc)
        gend = gc[:, CHUNK - 1:CHUNK, :]
        u = _bdot(tb, (v * beta).astype(BF16))
        w = _bdot(tb, (kb * eg).astype(BF16)).astype(BF16)
        qk = (_bdot_nt(q.astype(BF16), kbf) * gam).astype(BF16)
        qg = q * eg
        kd = (k * jnp.exp(gend - gc)).astype(BF16)
        ub = u.astype(BF16)
        pmat = _bdot_tn(kd, w).astype(BF16)
        nmat = _bdot_tn(kd, ub)
        qe = (qg - _bdot(qk, w)).astype(BF16)
        o0 = _bdot(qk, ub)
        for g in range(GDN_GROUP):
            cidx = ci * GDN_GROUP + g
            for h in range(N_HEADS):
                i = g * N_HEADS + h
                hs = _head_slice(h)
                qe_s[rows[g], hs] = qe[i]
                o0_s[rows[g], hs] = o0[i]
                p_s[cidx, hs, hs] = pmat[i]
                n_s[cidx, hs, hs] = nmat[i]
        return carry

    lax.fori_loop(0, t // (CHUNK * GDN_GROUP), factor_body, 0)

    lane_head = lax.broadcasted_iota(jnp.int32, (1, gw), 1) // HEAD_DIM
    onorm = on_ref[...]

    def scan_body(c, carry):
        r = pl.ds(pl.multiple_of(c * CHUNK, CHUNK), CHUNK)
        s_f = state[...]
        s_b = s_f.astype(BF16)
        o = _dot(qe_s[r, :], s_b) + o0_s[r, :]
        gend = g_s[r, :][CHUNK - 1:CHUNK, :]
        ge_row = jnp.zeros((1, gw), F32)
        for h in range(N_HEADS):
            ge_row = jnp.where(lane_head == h, jnp.exp(gend[:, N_HEADS + h:N_HEADS + h + 1]), ge_row)
        state[...] = s_f * ge_row - _dot(p_s[c], s_b) + n_s[c]
        ms = _dot01_right(o * o, head_ones) * (1.0 / HEAD_DIM)
        gate = p_ref[0, r, 3 * gw:4 * gw].astype(F32)
        o_ref[0, r, :] = (o * lax.rsqrt(ms + EPS) * onorm * _silu(gate)).astype(o_ref.dtype)
        return carry

    lax.fori_loop(0, t // CHUNK, scan_body, 0, unroll=8)


def _gdn(p, ps, cw, alog_row, dtb_row, onorm_t, ltri, t=512):
    b, s, _ = p.shape
    gw = GROUP_WIDTH
    const = lambda shape: pl.BlockSpec(shape, lambda i, j: (0,) * len(shape))
    return pl.pallas_call(
        _gdn_kernel,
        grid=(b, s // t),
        in_specs=[
            pl.BlockSpec((1, t, 4 * gw), lambda i, j: (i, j, 0)),
            pl.BlockSpec((1, t, SMALL_W), lambda i, j: (i, j, 0)),
            const((SHORT_CONV, 3 * gw)),
            const((1, SMALL_W)),
            const((1, SMALL_W)),
            const((1, gw)),
            const((CHUNK, CHUNK)),
        ],
        out_specs=pl.BlockSpec((1, t, gw), lambda i, j: (i, j, 0)),
        out_shape=jax.ShapeDtypeStruct((b, s, gw), BF16),
        scratch_shapes=[
            pltpu.VMEM((t + HALO, 3 * gw), F32),
            pltpu.VMEM((t, gw), F32),
            pltpu.VMEM((t, gw), F32),
            pltpu.VMEM((t, gw), F32),
            pltpu.VMEM((t, SMALL_W), F32),
            pltpu.VMEM((t, SMALL_W), F32),
            pltpu.VMEM((t, SMALL_W), F32),
            pltpu.VMEM((t, gw), BF16),
            pltpu.VMEM((t, gw), F32),
            pltpu.VMEM((t // CHUNK, gw, gw), BF16),
            pltpu.VMEM((t // CHUNK, gw, gw), F32),
            pltpu.VMEM((gw, gw), F32),
        ],
        compiler_params=pltpu.CompilerParams(
            dimension_semantics=("parallel", "arbitrary"), vmem_limit_bytes=VMEM_LIMIT),
        name="gdn",
    )(p, ps, cw, alog_row, dtb_row, onorm_t, ltri)


def _ret_kernel(p_ref, cos_ref, sin_ref, dmat_ref, wst_ref, wend_ref, gdec_ref, on_ref,
                o_ref, state):
    gw = GROUP_WIDTH

    @pl.when(pl.program_id(1) == 0)
    def _():
        state[...] = jnp.zeros_like(state)

    q = p_ref[0, :, 0:gw].astype(F32)
    k = p_ref[0, :, gw:2 * gw].astype(F32)
    lane = lax.broadcasted_iota(jnp.int32, q.shape, 1)
    first_half = (lane % HEAD_DIM) < (HEAD_DIM // 2)
    cos = cos_ref[...]
    sin = sin_ref[...]

    def rope(x):
        rot = jnp.where(first_half, pltpu.roll(x, gw - HEAD_DIM // 2, 1), pltpu.roll(x, HEAD_DIM // 2, 1))
        return x * cos + rot * sin

    q = rope(q)
    k = rope(k) * HEAD_DIM ** -0.5
    qs = (q * wst_ref[...]).astype(BF16)
    ke = (k * wend_ref[...]).astype(BF16)
    qb = q.astype(BF16)
    kb = k.astype(BF16)
    onorm = on_ref[...]
    for h in range(N_HEADS):
        hs = _head_slice(h)
        v = p_ref[0, :, 2 * gw + h * HEAD_DIM:2 * gw + (h + 1) * HEAD_DIM]
        scores = (_dot_nt(qb[:, hs], kb[:, hs]) * dmat_ref[h]).astype(BF16)
        s_h = state[h]
        o = _dot(scores, v) + _dot(qs[:, hs], s_h.astype(BF16))
        state[h] = gdec_ref[h] * s_h + _dot_tn(ke[:, hs], v)
        oc = o - jnp.mean(o, axis=-1, keepdims=True)
        o = oc * lax.rsqrt(jnp.mean(oc * oc, axis=-1, keepdims=True) + EPS) * onorm
        gate = p_ref[0, :, 3 * gw + h * HEAD_DIM:3 * gw + (h + 1) * HEAD_DIM].astype(F32)
        o_ref[0, :, hs] = (o * _silu(gate)).astype(o_ref.dtype)


def _ret(p, cos, sin, dmat, wst, wend, gdec, onorm, t):
    b, s, _ = p.shape
    gw = GROUP_WIDTH
    return pl.pallas_call(
        _ret_kernel,
        grid=(b, s // t),
        in_specs=[
            pl.BlockSpec((1, t, 4 * gw), lambda i, j: (i, j, 1)),
            pl.BlockSpec((t, gw), lambda i, j: (j, 0)),
            pl.BlockSpec((t, gw), lambda i, j: (j, 0)),
            pl.BlockSpec((N_HEADS, t, t), lambda i, j: (0, 0, 0)),
            pl.BlockSpec((t, gw), lambda i, j: (0, 0)),
            pl.BlockSpec((t, gw), lambda i, j: (0, 0)),
            pl.BlockSpec(memory_space=pltpu.SMEM),
            pl.BlockSpec((1, HEAD_DIM), lambda i, j: (0, 0)),
        ],
        out_specs=pl.BlockSpec((1, t, gw), lambda i, j: (i, j, 0)),
        out_shape=jax.ShapeDtypeStruct((b, s, gw), BF16),
        scratch_shapes=[pltpu.VMEM((N_HEADS, HEAD_DIM, HEAD_DIM), F32)],
        compiler_params=pltpu.CompilerParams(
            dimension_semantics=("parallel", "arbitrary"), vmem_limit_bytes=VMEM_LIMIT),
        name="retention",
    )(p, cos, sin, dmat, wst, wend, gdec, onorm)


def _fox_prep_kernel(p_ref, ps_ref, qn_ref, kn_ref, fb_ref, utri_ref,
                     q_out, k_out, v_out, c_out, carry):
    gw = GROUP_WIDTH

    @pl.when(pl.program_id(1) == 0)
    def _():
        carry[...] = jnp.zeros_like(carry)

    brow = lax.broadcasted_iota(jnp.int32, (gw, gw), 0) // HEAD_DIM
    bcol = lax.broadcasted_iota(jnp.int32, (gw, gw), 1) // HEAD_DIM
    head_ones = (brow == bcol).astype(BF16)
    q = p_ref[0, :, 0:gw].astype(F32)
    k = p_ref[0, :, gw:2 * gw].astype(F32)
    q = q * lax.rsqrt(_dot01_right(q * q, head_ones) * (1.0 / HEAD_DIM) + EPS) * qn_ref[...] * HEAD_DIM ** -0.5
    k = k * lax.rsqrt(_dot01_right(k * k, head_ones) * (1.0 / HEAD_DIM) + EPS) * kn_ref[...]
    qb = q.astype(BF16)
    kb = k.astype(BF16)
    ones_col = (lax.broadcasted_iota(jnp.int32, (p_ref.shape[1], HEAD_DIM), 1) == 0).astype(BF16)
    for h in range(N_HEADS):
        q_out[0, h] = qb[:, _head_slice(h)]
        k_out[0, h] = kb[:, _head_slice(h)]
        v_out[0, h, :, 0:HEAD_DIM] = p_ref[0, :, 2 * gw + h * HEAD_DIM:2 * gw + (h + 1) * HEAD_DIM]
        v_out[0, h, :, HEAD_DIM:2 * HEAD_DIM] = ones_col

    x = ps_ref[0] + fb_ref[...]
    logf = jnp.minimum(x, 0.0) - jnp.log1p(jnp.exp(-jnp.abs(x)))
    logf_t = logf.T[8:16, :]
    c = _dot(logf_t, utri_ref[...], HI) + carry[:, 0:1]
    c_out[0, 0] = c
    t = c.shape[1]
    carry[...] = jnp.broadcast_to(c[:, t - 1:t], carry.shape)


def _fox_prep(p, ps, qn, kn, fb_row, utri, t):
    b, s, _ = p.shape
    gw = GROUP_WIDTH
    hm = jax.ShapeDtypeStruct((b, N_HEADS, s, HEAD_DIM), BF16)
    hm_spec = pl.BlockSpec((1, N_HEADS, t, HEAD_DIM), lambda i, j: (i, 0, j, 0))
    hv = jax.ShapeDtypeStruct((b, N_HEADS, s, 2 * HEAD_DIM), BF16)
    hv_spec = pl.BlockSpec((1, N_HEADS, t, 2 * HEAD_DIM), lambda i, j: (i, 0, j, 0))
    return pl.pallas_call(
        _fox_prep_kernel,
        grid=(b, s // t),
        in_specs=[
            pl.BlockSpec((1, t, 4 * gw), lambda i, j: (i, j, 2)),
            pl.BlockSpec((1, t, SMALL_W), lambda i, j: (i, j, 0)),
            pl.BlockSpec((1, gw), lambda i, j: (0, 0)),
            pl.BlockSpec((1, gw), lambda i, j: (0, 0)),
            pl.BlockSpec((1, SMALL_W), lambda i, j: (0, 0)),
            pl.BlockSpec((t, t), lambda i, j: (0, 0)),
        ],
        out_specs=[hm_spec, hm_spec, hv_spec, pl.BlockSpec((1, 1, 8, t), lambda i, j: (i, j, 0, 0))],
        out_shape=[hm, hm, hv, jax.ShapeDtypeStruct((b, s // t, 8, t), F32)],
        scratch_shapes=[pltpu.VMEM((8, 128), F32)],
        compiler_params=pltpu.CompilerParams(
            dimension_semantics=("parallel", "arbitrary"), vmem_limit_bytes=VMEM_LIMIT),
        name="fox_prep",
    )(p, ps, qn, kn, fb_row, utri)


FOX_STRIP = 32
LANES = 128


def _fox_kernel(q_ref, k_ref, v_ref, c_ref, g_ref, o_ref, m_s, acc_s, s_scr, p_scr, al_scr):
    qi = pl.program_id(1)
    tq = q_ref.shape[2]
    tk = c_ref.shape[3]
    nj = tk // LANES

    m_s[...] = jnp.full_like(m_s, NEG_BIG)
    acc_s[...] = jnp.zeros_like(acc_s)

    def update(ki, masked, slot):
        kr = pl.ds(pl.multiple_of(ki * tk, tk), tk)
        for h in range(N_HEADS):
            s_scr[slot, h] = _dot_nt(q_ref[0, h], k_ref[0, h, kr, :])
        row = lax.broadcasted_iota(jnp.int32, (FOX_STRIP, LANES), 0)
        col = lax.broadcasted_iota(jnp.int32, (FOX_STRIP, LANES), 1)
        for h in range(N_HEADS):
            c_blk = [c_ref[0, ki, h:h + 1, j * LANES:(j + 1) * LANES] for j in range(nj)]
            for i in range(tq // FOX_STRIP):
                r0 = i * FOX_STRIP
                r = slice(r0, r0 + FOX_STRIP)
                live = min(nj, (r0 + FOX_STRIP - 1) // LANES + 1) if masked else nj
                sb = [s_scr[slot, h, r, j * LANES:(j + 1) * LANES] - c_blk[j] for j in range(live)]
                if masked:
                    sb = [jnp.where(row + r0 >= col + j * LANES, sb[j], NEG_BIG)
                          if (j + 1) * LANES - 1 > r0 else sb[j] for j in range(live)]
                mx = sb[0]
                for j in range(1, live):
                    mx = jnp.maximum(mx, sb[j])
                m_old = m_s[h, r, :]
                m_new = jnp.maximum(m_old, jnp.max(mx, axis=-1, keepdims=True))
                m_s[h, r, :] = m_new
                al_scr[slot, h, r, :] = jnp.exp(m_old - m_new)
                for j in range(live):
                    p_scr[slot, h, r, j * LANES:(j + 1) * LANES] = jnp.exp(sb[j] - m_new).astype(BF16)
                for j in range(live, nj):
                    p_scr[slot, h, r, j * LANES:(j + 1) * LANES] = jnp.zeros((FOX_STRIP, LANES), BF16)
        for h in range(N_HEADS):
            acc_s[h] = al_scr[slot, h] * acc_s[h] + _dot(p_scr[slot, h], v_ref[0, h, kr, :])

    def below_diagonal_pair(pi, carry):
        update(2 * pi, False, 0)
        update(2 * pi + 1, False, 1)
        return carry

    lax.fori_loop(0, qi // 2, below_diagonal_pair, 0)

    @pl.when(qi % 2 == 1)
    def _():
        update(qi - 1, False, 0)

    update(qi, True, 1)
    for h in range(N_HEADS):
        hs = _head_slice(h)
        acc = acc_s[h]
        o = acc[:, 0:HEAD_DIM] / acc[:, HEAD_DIM:HEAD_DIM + 1]
        gate = g_ref[0, :, hs].astype(F32)
        o_ref[0, :, hs] = (o * _sigmoid(gate)).astype(o_ref.dtype)


def _fox(qh, kh, vh, c, p, t):
    b, _, s, _ = qh.shape
    gw = GROUP_WIDTH
    n = s // t
    return pl.pallas_call(
        _fox_kernel,
        grid=(b, n),
        in_specs=[
            pl.BlockSpec((1, N_HEADS, t, HEAD_DIM), lambda i, j: (i, 0, j, 0)),
            pl.BlockSpec((1, N_HEADS, s, HEAD_DIM), lambda i, j: (i, 0, 0, 0), pipeline_mode=pl.Buffered(1)),
            pl.BlockSpec((1, N_HEADS, s, 2 * HEAD_DIM), lambda i, j: (i, 0, 0, 0), pipeline_mode=pl.Buffered(1)),
            pl.BlockSpec((1, n, 8, t), lambda i, j: (i, 0, 0, 0), pipeline_mode=pl.Buffered(1)),
            pl.BlockSpec((1, t, gw), lambda i, j: (i, j, 4 * 2 + 3)),
        ],
        out_specs=pl.BlockSpec((1, t, gw), lambda i, j: (i, j, 0)),
        out_shape=jax.ShapeDtypeStruct((b, s, gw), BF16),
        scratch_shapes=[
            pltpu.VMEM((N_HEADS, t, LANES), F32),
            pltpu.VMEM((N_HEADS, t, 2 * HEAD_DIM), F32),
            pltpu.VMEM((2, N_HEADS, t, t), F32),
            pltpu.VMEM((2, N_HEADS, t, t), BF16),
            pltpu.VMEM((2, N_HEADS, t, LANES), F32),
        ],
        compiler_params=pltpu.CompilerParams(
            dimension_semantics=("parallel", "parallel"), vmem_limit_bytes=VMEM_LIMIT),
        name="fox_attn",
    )(qh, kh, vh, c, p)


N_LEVELS = 6


def _hgrn_tables():
    c = CHUNK
    mall = np.zeros((N_LEVELS + 2, c, c), np.float32)
    masks = np.zeros((N_LEVELS + 1, c, c), np.float32)
    for lv in range(N_LEVELS):
        half = 1 << lv
        for t in range(c):
            m = (t >> (lv + 1)) * (2 * half) + half
            if t >= m:
                mall[lv, t, m:t + 1] = 1.0
            else:
                mall[lv, t, t + 1:m] = 1.0
            for s in range(c):
                if (s >> (lv + 1)) == (t >> (lv + 1)) and t >= m and s < m:
                    masks[lv, t, s] = 1.0
    masks[N_LEVELS] = np.eye(c, dtype=np.float32)
    for t in range(c):
        mall[N_LEVELS, t, :t + 1] = 1.0
        mall[N_LEVELS + 1, t, t + 1:] = 1.0
    return mall.reshape((N_LEVELS + 2) * c, c), masks


def _hgrn_kernel(p_ref, lb_ref, on_ref, mall_ref, mask_ref, o_ref, state):
    t = p_ref.shape[1]
    gw = GROUP_WIDTH

    @pl.when(pl.program_id(1) == 0)
    def _():
        state[...] = jnp.zeros_like(state)

    lb = lb_ref[...]
    onorm = on_ref[...]
    mall = mall_ref[...]
    nl = N_LEVELS + 1
    brow = lax.broadcasted_iota(jnp.int32, (gw, gw), 0) // HEAD_DIM
    bcol = lax.broadcasted_iota(jnp.int32, (gw, gw), 1) // HEAD_DIM
    same_head = brow == bcol
    head_ones = same_head.astype(BF16)
    zero_b = jnp.zeros((), BF16)

    def per_head_rows(x):
        return jnp.where(same_head, jnp.concatenate([x] * N_HEADS, axis=0), zero_b)

    def chunk_body(c, carry):
        r = pl.ds(pl.multiple_of(c * CHUNK, CHUNK), CHUNK)
        qx = p_ref[0, r, 0:gw].astype(F32)
        f = p_ref[0, r, gw:2 * gw].astype(F32)
        logf = jnp.log(lb + (1.0 - lb) * _sigmoid(f))
        kk = (1.0 - lb) * _sigmoid(-f)
        qq = _silu(qx)
        x_all = jnp.exp(_dot(mall, logf.astype(BF16)))
        x_q = x_all[N_LEVELS * CHUNK:(N_LEVELS + 1) * CHUNK, :]
        x_k = x_all[(N_LEVELS + 1) * CHUNK:(N_LEVELS + 2) * CHUNK, :]
        scores = None
        for lv in range(nl):
            if lv < N_LEVELS:
                x_l = x_all[lv * CHUNK:(lv + 1) * CHUNK, :]
                q_l = (qq * x_l).astype(BF16)
                k_l = (kk * x_l).astype(BF16)
            else:
                q_l = qq.astype(BF16)
                k_l = kk.astype(BF16)
            term = _dot_nt(q_l, per_head_rows(k_l)) * mask_ref[lv]
            scores = term if scores is None else scores + term
        v = p_ref[0, r, 2 * gw:3 * gw]
        st = state[...]
        o = _dot(scores.astype(BF16), per_head_rows(v)) + _dot_nt((qq * x_q).astype(BF16), st.astype(BF16))
        state[...] = st * x_q[CHUNK - 1:CHUNK, :] + jnp.where(
            same_head, _dot_tn(v, (kk * x_k).astype(BF16)), 0.0)
        ms = _dot01_right(o * o, head_ones) * (1.0 / HEAD_DIM)
        gate = p_ref[0, r, 3 * gw:4 * gw].astype(F32)
        o_ref[0, r, :] = (o * lax.rsqrt(ms + EPS) * onorm * _silu(gate)).astype(o_ref.dtype)
        return carry

    lax.fori_loop(0, t // CHUNK, chunk_body, 0, unroll=8)


def _hgrn(p, lb_row, onorm, mall, masks, t=512):
    b, s, _ = p.shape
    gw = GROUP_WIDTH
    return pl.pallas_call(
        _hgrn_kernel,
        grid=(b, s // t),
        in_specs=[
            pl.BlockSpec((1, t, 4 * gw), lambda i, j: (i, j, 3)),
            pl.BlockSpec((1, gw), lambda i, j: (0, 0)),
            pl.BlockSpec((1, gw), lambda i, j: (0, 0)),
            pl.BlockSpec(mall.shape, lambda i, j: (0, 0)),
            pl.BlockSpec(masks.shape, lambda i, j: (0, 0, 0)),
        ],
        out_specs=pl.BlockSpec((1, t, gw), lambda i, j: (i, j, 0)),
        out_shape=jax.ShapeDtypeStruct((b, s, gw), BF16),
        scratch_shapes=[pltpu.VMEM((gw, gw), F32)],
        compiler_params=pltpu.CompilerParams(
            dimension_semantics=("parallel", "arbitrary"), vmem_limit_bytes=VMEM_LIMIT),
        name="hgrn2",
    )(p, lb_row, onorm, mall, masks)


def _post_kernel(h_ref, ya_ref, yb_ref, yc_ref, yd_ref, wo_ref, g_ref, wg_ref, wv_ref,
                 cg_ref, cv_ref, wd_ref, o_ref, buf_g, buf_v, un_s, act_s, *, tf):
    tm = h_ref.shape[1]
    gw = GROUP_WIDTH
    d_ff = wg_ref.shape[1]

    @pl.when(pl.program_id(1) == 0)
    def _():
        buf_g[0:HALO, :] = jnp.zeros((HALO, d_ff), F32)
        buf_v[0:HALO, :] = jnp.zeros((HALO, d_ff), F32)

    h1 = h_ref[0]
    for g, y_ref in enumerate((ya_ref, yb_ref, yc_ref, yd_ref)):
        h1 = h1 + _dot(y_ref[0], wo_ref[g * gw:(g + 1) * gw, :])
    ms = jnp.mean(h1 * h1, axis=-1, keepdims=True)
    un_s[...] = (h1 * lax.rsqrt(ms + EPS) * g_ref[...]).astype(BF16)
    o_ref[0] = h1

    def conv(up, buf, cw, fs):
        buf[HALO:HALO + tm, fs] = up
        out = cw[FFN_CONV - 1:FFN_CONV, fs] * up
        for j in range(FFN_CONV - 1):
            off = HALO - (FFN_CONV - 1) + j
            out = out + cw[j:j + 1, fs] * buf[off:off + tm, fs]
        buf[0:HALO, fs] = up[tm - HALO:tm, :]
        return out

    n_f = d_ff // tf
    split = (n_f // 2) * tf
    for f in range(n_f):
        fs = slice(f * tf, (f + 1) * tf)
        cg = conv(_dot(un_s[...], wg_ref[:, fs]), buf_g, cg_ref, fs)
        cv = conv(_dot(un_s[...], wv_ref[:, fs]), buf_v, cv_ref, fs)
        act_s[:, fs] = (_silu(cg) * cv).astype(BF16)
        if (f + 1) * tf == split:
            o_ref[0] += _dot(act_s[:, 0:split], wd_ref[0:split, :])
    o_ref[0] += _dot(act_s[:, split:d_ff], wd_ref[split:d_ff, :])


def _post(h, ya, yb, yc, yd, wo, g, wg, wv, cg, cv, wd, tm=512, tf=256):
    b, s, d = h.shape
    gw = GROUP_WIDTH
    d_ff = wg.shape[1]
    const = lambda shape: pl.BlockSpec(shape, lambda i, j: (0,) * len(shape))
    y_spec = pl.BlockSpec((1, tm, gw), lambda i, j: (i, j, 0))
    return pl.pallas_call(
        partial(_post_kernel, tf=tf),
        grid=(b, s // tm),
        in_specs=[
            pl.BlockSpec((1, tm, d), lambda i, j: (i, j, 0)),
            y_spec, y_spec, y_spec, y_spec,
            const((N_GROUPS * gw, d)),
            const((1, d)),
            const((d, d_ff)),
            const((d, d_ff)),
            const((FFN_CONV, d_ff)),
            const((FFN_CONV, d_ff)),
            const((d_ff, d)),
        ],
        out_specs=pl.BlockSpec((1, tm, d), lambda i, j: (i, j, 0)),
        out_shape=jax.ShapeDtypeStruct((b, s, d), F32),
        scratch_shapes=[
            pltpu.VMEM((tm + HALO, d_ff), F32),
            pltpu.VMEM((tm + HALO, d_ff), F32),
            pltpu.VMEM((tm, d), BF16),
            pltpu.VMEM((tm, d_ff), BF16),
        ],
        compiler_params=pltpu.CompilerParams(
            dimension_semantics=("parallel", "arbitrary"), vmem_limit_bytes=VMEM_LIMIT),
        name="post_ffn",
    )(h, ya, yb, yc, yd, wo, g, wg, wv, cg, cv, wd)


def _retention_tables(seq, t):
    hd = HEAD_DIM
    inv_freq = ROPE_BASE ** (-jnp.arange(0, hd, 2, dtype=F32) / hd)
    ang = jnp.arange(seq, dtype=F32)[:, None] * inv_freq[None, :]
    cos, sin = jnp.cos(ang), jnp.sin(ang)
    cos_t = jnp.tile(jnp.concatenate([cos, cos], axis=-1), (1, N_HEADS))
    sin_t = jnp.tile(jnp.concatenate([-sin, sin], axis=-1), (1, N_HEADS))
    lgh = jnp.log1p(-jnp.exp2(-RET_DECAY_EXP - jnp.arange(N_HEADS, dtype=F32)))
    n = jnp.arange(t, dtype=F32)
    diff = n[:, None] - n[None, :]
    keep = diff >= 0
    dmat = jnp.where(keep[None], jnp.exp(jnp.where(keep, diff, 0.0)[None] * lgh[:, None, None]), 0.0)
    wst = jnp.repeat(jnp.exp((n + 1.0)[:, None] * lgh[None, :]), hd, axis=1)
    wend = jnp.repeat(jnp.exp((t - 1.0 - n)[:, None] * lgh[None, :]), hd, axis=1)
    gdec = jnp.exp(t * lgh)
    return cos_t, sin_t, dmat, wst, wend, gdec


def kernel(x, norm_mix, norm_ffn, w_in, conv_qkv_a, a_log_a, dt_bias_a, onorm_a, onorm_b, qnorm_c, knorm_c,
           fbias_c, lower_bound_d, onorm_d, w_out, w_up, conv_ffn, w_down):
    b, s, d = x.shape
    depth = w_in.shape[0]
    gw = GROUP_WIDTH
    nh = N_HEADS
    d_ff = w_down.shape[1]
    t_ret = 512
    t_fox = 512

    oa = 4 * gw
    ob = oa + 2 * nh
    oc = ob + 4 * gw
    od = oc + 4 * gw + nh
    w_main = jnp.concatenate(
        [w_in[:, :, 0:oa], w_in[:, :, ob:ob + 4 * gw], w_in[:, :, oc:oc + 4 * gw], w_in[:, :, od:od + 4 * gw]],
        axis=-1).astype(BF16)
    w_small = jnp.concatenate([w_in[:, :, oa:ob], w_in[:, :, oc + 4 * gw:od]], axis=-1)
    w_small = jnp.pad(w_small, ((0, 0), (0, 0), (0, SMALL_W - 3 * nh))).astype(BF16)
    w_out_b = w_out.astype(BF16)
    w_g = w_up[:, :, :d_ff].astype(BF16)
    w_v = w_up[:, :, d_ff:].astype(BF16)
    w_down_b = w_down.astype(BF16)

    def small_row(vals, off):
        return jnp.zeros((depth, 1, SMALL_W), F32).at[:, 0, off:off + nh].set(vals.astype(F32))

    alog_rows = small_row(a_log_a, nh)
    dtb_rows = small_row(dt_bias_a, nh)
    fb_rows = small_row(fbias_c, 2 * nh)

    lbs = jax.nn.softmax(lower_bound_d.astype(F32), axis=0)
    lbs = jnp.cumsum(lbs, axis=0) - lbs[0]

    ltri = jnp.asarray(np.tril(np.ones((CHUNK, CHUNK), np.float32))).astype(BF16)
    utri_fox = jnp.asarray(np.triu(np.ones((t_fox, t_fox), np.float32)))
    mall_np, masks_np = _hgrn_tables()
    mall = jnp.asarray(mall_np).astype(BF16)
    masks = jnp.asarray(np.tile(masks_np, (1, 1, nh)))
    cos_t, sin_t, dmat, wst, wend, gdec = _retention_tables(s, t_ret)

    h = x.astype(F32)
    for l in range(depth):
        pm, ps = _inproj(h.reshape(b * s, d), norm_mix[l].reshape(1, d).astype(F32), w_main[l], w_small[l])
        pm = pm.reshape(b, s, 4 * 4 * gw)
        ps = ps.reshape(b, s, SMALL_W)
        ya = _gdn(pm, ps, conv_qkv_a[l].astype(F32), alog_rows[l], dtb_rows[l],
                  jnp.tile(onorm_a[l].reshape(1, HEAD_DIM).astype(F32), (1, nh)), ltri)
        yb = _ret(pm, cos_t, sin_t, dmat, wst, wend, gdec, onorm_b[l].reshape(1, HEAD_DIM).astype(F32), t_ret)
        qh, kh, vh, c = _fox_prep(pm, ps, jnp.tile(qnorm_c[l].reshape(1, HEAD_DIM).astype(F32), (1, nh)),
                                  jnp.tile(knorm_c[l].reshape(1, HEAD_DIM).astype(F32), (1, nh)),
                                  fb_rows[l], utri_fox, t_fox)
        yc = _fox(qh, kh, vh, c, pm, t_fox)
        yd = _hgrn(pm, lbs[l].reshape(1, gw), jnp.tile(onorm_d[l].reshape(1, HEAD_DIM).astype(F32), (1, nh)),
                   mall, masks)
        h = _post(h, ya, yb, yc, yd, w_out_b[l], norm_ffn[l].reshape(1, d).astype(F32), w_g[l], w_v[l],
                  conv_ffn[l][:, :d_ff].astype(F32), conv_ffn[l][:, d_ff:].astype(F32), w_down_b[l])
    return h.astype(x.dtype)
```
